```python
import jax
import jax.numpy as jnp
from jax import lax
import numpy as np

D_MODEL = 2048
BATCH = 2
SEQ = 8192
DEPTH = 4

HEAD_DIM = 128
ROPE_THETA = 10000.0
DSA_HEADS = 8
MOBA_HEADS = 8
IDX_HEADS = 16
IDX_DIM = 64
DSA_TOPK_MAX = 256
MOBA_BLOCK = 256
MOBA_TOPB_MAX = 3
MLA_HEADS = 16
Q_LORA = 512
KV_LORA = 512
QK_NOPE = 128
QK_ROPE = 64
V_DIM = 128
N_EXPERTS = 32
TOP_K = 4
EXPERT_DIM = 512
SWIGLU_LIMIT = 7.0
SWIGLU_ALPHA = 1.702
Q_BLOCK = 128
MOBA_Q_BLOCK = 64
N_MOD = 6
DEEPNORM_ALPHA = (2 * DEPTH) ** 0.25
DEEPNORM_BETA = (8 * DEPTH) ** -0.25
N_EVEN = (DEPTH + 1) // 2
N_ODD = DEPTH // 2
EVEN_SPLITS = (DSA_HEADS * HEAD_DIM, DSA_HEADS * HEAD_DIM, DSA_HEADS * HEAD_DIM,
               IDX_HEADS * IDX_DIM, IDX_DIM, IDX_HEADS,
               MOBA_HEADS * HEAD_DIM, MOBA_HEADS * HEAD_DIM, MOBA_HEADS * HEAD_DIM)
EVEN_IN = sum(EVEN_SPLITS)
EVEN_OUT = (DSA_HEADS + MOBA_HEADS) * HEAD_DIM
MLA_DOWN = Q_LORA + KV_LORA + QK_ROPE
NEG = -1e30

kernel_name = 'hybrid_dsa_moba_mla_moe_deepnorm_adaln'


def split_cols(a, sizes):
    offsets = [int(o) for o in np.cumsum(sizes)[:-1]]
    return jnp.split(a, offsets, axis=-1)


def layer_norm(x, g, b, eps=1e-5):
    xf = x.astype(jnp.float32)
    mu = jnp.mean(xf, axis=-1, keepdims=True)
    var = jnp.mean(jnp.square(xf - mu), axis=-1, keepdims=True)
    return ((xf - mu) * lax.rsqrt(var + eps) * g + b).astype(x.dtype)


def rms_norm(x, g, eps=1e-6):
    xf = x.astype(jnp.float32)
    return (xf * lax.rsqrt(jnp.mean(xf * xf, axis=-1, keepdims=True) + eps) * g).astype(x.dtype)


def rope_tables(seq, dim):
    inv = 1.0 / (ROPE_THETA ** (jnp.arange(0, dim, 2, dtype=jnp.float32) / dim))
    ang = jnp.arange(seq, dtype=jnp.float32)[:, None] * inv[None, :]
    return jnp.cos(ang), jnp.sin(ang)


def apply_rope(x, cos, sin):
    c = cos[None, :, None, :].astype(x.dtype)
    s = sin[None, :, None, :].astype(x.dtype)
    x1, x2 = jnp.split(x, 2, axis=-1)
    return jnp.concatenate([x1 * c - x2 * s, x1 * s + x2 * c], axis=-1)


def dsa_attention(q, k, v, q_idx, k_idx, w_idx):
    B, S, H, D = q.shape
    topk = min(DSA_TOPK_MAX, S // 4)
    scale = HEAD_DIM ** -0.5
    idx_scale = IDX_DIM ** -0.5
    w_scaled = w_idx.astype(jnp.float32) * (IDX_HEADS ** -0.5)
    key_pos = jnp.arange(S)
    gather = jax.vmap(lambda kk, ii: kk[ii])

    def block(i):
        t0 = i * Q_BLOCK
        qb = lax.dynamic_slice_in_dim(q, t0, Q_BLOCK, axis=1)
        qib = lax.dynamic_slice_in_dim(q_idx, t0, Q_BLOCK, axis=1)
        wb = lax.dynamic_slice_in_dim(w_scaled, t0, Q_BLOCK, axis=1)
        q_pos = t0 + jnp.arange(Q_BLOCK)
        causal = key_pos[None, :] <= q_pos[:, None]
        logits = jnp.einsum('bthd,bsd->bths', qib, k_idx).astype(jnp.float32) * idx_scale
        score = jnp.einsum('bth,bths->bts', wb, jax.nn.relu(logits))
        score = jnp.where(causal[None], score, -jnp.inf)
        _, sel = lax.top_k(score, topk)
        valid = sel <= q_pos[None, :, None]
        kg = gather(k, sel)
        vg = gather(v, sel)
        s = jnp.einsum('bthd,btkhd->bhtk', qb, kg).astype(jnp.float32) * scale
        s = jnp.where(valid[:, None], s, NEG)
        p = jax.nn.softmax(s, axis=-1).astype(v.dtype)
        return jnp.einsum('bhtk,btkhd->bthd', p, vg)

    out = lax.map(block, jnp.arange(S // Q_BLOCK))
    return out.transpose(1, 0, 2, 3, 4).reshape(B, S, H * D)


def moba_attention(q, k, v):
    B, S, H, D = q.shape
    nb = -(-S // MOBA_BLOCK)
    pad = nb * MOBA_BLOCK - S
    topb = min(MOBA_TOPB_MAX, nb)
    scale = HEAD_DIM ** -0.5
    padw = ((0, 0), (0, pad), (0, 0), (0, 0))
    kbh = jnp.pad(k, padw).reshape(B, nb, MOBA_BLOCK, H, D).transpose(0, 3, 1, 2, 4)
    vbh = jnp.pad(v, padw).reshape(B, nb, MOBA_BLOCK, H, D).transpose(0, 3, 1, 2, 4)
    kmean = jnp.mean(kbh, axis=3)
    qbh = q.transpose(0, 2, 1, 3)
    blk_ids = jnp.arange(nb)
    in_blk = jnp.arange(MOBA_BLOCK)
    gather = jax.vmap(jax.vmap(lambda kk, ii: kk[ii]))

    def block(i):
        t0 = i * MOBA_Q_BLOCK
        qb = lax.dynamic_slice_in_dim(qbh, t0, MOBA_Q_BLOCK, axis=2)
        q_pos = t0 + jnp.arange(MOBA_Q_BLOCK)
        own = t0 // MOBA_BLOCK
        gate = jnp.einsum('bhtd,bhnd->bhtn', qb, kmean).astype(jnp.float32)
        gate = jnp.where((blk_ids < own)[None, None, None, :], gate, -jnp.inf)
        _, sel = lax.top_k(gate, topb)
        sel_valid = sel < own
        kg = gather(kbh, sel)
        vg = gather(vbh, sel)
        s_sel = jnp.einsum('bhtd,bhtnkd->bhtnk', qb, kg).astype(jnp.float32) * scale
        s_sel = jnp.where(sel_valid[..., None], s_sel, NEG).reshape(B, H, MOBA_Q_BLOCK, topb * MOBA_BLOCK)
        k_own = lax.dynamic_slice_in_dim(kbh, own, 1, axis=2)[:, :, 0]
        v_own = lax.dynamic_slice_in_dim(vbh, own, 1, axis=2)[:, :, 0]
        s_own = jnp.einsum('bhtd,bhkd->bhtk', qb, k_own).astype(jnp.float32) * scale
        own_causal = (own * MOBA_BLOCK + in_blk)[None, :] <= q_pos[:, None]
        s_own = jnp.where(own_causal[None, None], s_own, NEG)
        p = jax.nn.softmax(jnp.concatenate([s_sel, s_own], axis=-1), axis=-1).astype(v.dtype)
        p_sel = p[..., :topb * MOBA_BLOCK].reshape(B, H, MOBA_Q_BLOCK, topb, MOBA_BLOCK)
        p_own = p[..., topb * MOBA_BLOCK:]
        return (jnp.einsum('bhtnk,bhtnkd->bhtd', p_sel, vg)
                + jnp.einsum('bhtk,bhkd->bhtd', p_own, v_own))

    out = lax.map(block, jnp.arange(S // MOBA_Q_BLOCK))
    return out.transpose(1, 0, 3, 2, 4).reshape(B, S, H * D)


def even_mixer(h, w_in, w_out, k_ln_g, k_ln_b, cos_h, sin_h, cos_i, sin_i):
    B, S, _ = h.shape
    qa, ka, va, qi, ki, wi, qb, kb, vb = split_cols(h @ w_in, EVEN_SPLITS)
    qa = apply_rope(qa.reshape(B, S, DSA_HEADS, HEAD_DIM), cos_h, sin_h)
    ka = apply_rope(ka.reshape(B, S, DSA_HEADS, HEAD_DIM), cos_h, sin_h)
    va = va.reshape(B, S, DSA_HEADS, HEAD_DIM)
    qi = apply_rope(qi.reshape(B, S, IDX_HEADS, IDX_DIM), cos_i, sin_i)
    ki = apply_rope(layer_norm(ki, k_ln_g, k_ln_b)[:, :, None, :], cos_i, sin_i)[:, :, 0]
    qb = apply_rope(qb.reshape(B, S, MOBA_HEADS, HEAD_DIM), cos_h, sin_h)
    kb = apply_rope(kb.reshape(B, S, MOBA_HEADS, HEAD_DIM), cos_h, sin_h)
    vb = vb.reshape(B, S, MOBA_HEADS, HEAD_DIM)
    out_a = dsa_attention(qa, ka, va, qi, ki, wi)
    out_b = moba_attention(qb, kb, vb)
    return jnp.concatenate([out_a, out_b], axis=-1) @ w_out


def mla_mixer(h, w_down, q_norm_g, kv_norm_g, w_q_up, w_kv_up, w_out, cos_r, sin_r):
    B, S, _ = h.shape
    cq, ckv, k_rope = split_cols(h @ w_down, (Q_LORA, KV_LORA, QK_ROPE))
    q = (rms_norm(cq, q_norm_g) @ w_q_up).reshape(B, S, MLA_HEADS, QK_NOPE + QK_ROPE)
    q_nope = q[..., :QK_NOPE]
    q_rope = apply_rope(q[..., QK_NOPE:], cos_r, sin_r)
    k_rope = apply_rope(k_rope[:, :, None, :], cos_r, sin_r)[:, :, 0]
    kv = (rms_norm(ckv, kv_norm_g) @ w_kv_up).reshape(B, S, MLA_HEADS, QK_NOPE + V_DIM)
    k_nope = kv[..., :QK_NOPE]
    v = kv[..., QK_NOPE:]
    scale = (QK_NOPE + QK_ROPE) ** -0.5
    key_pos = jnp.arange(S)

    def block(i):
        t0 = i * Q_BLOCK
        qn = lax.dynamic_slice_in_dim(q_nope, t0, Q_BLOCK, axis=1)
        qr = lax.dynamic_slice_in_dim(q_rope, t0, Q_BLOCK, axis=1)
        q_pos = t0 + jnp.arange(Q_BLOCK)
        s = (jnp.einsum('bthd,bshd->bhts', qn, k_nope)
             + jnp.einsum('bthd,bsd->bhts', qr, k_rope)).astype(jnp.float32) * scale
        s = jnp.where((key_pos[None, :] <= q_pos[:, None])[None, None], s, NEG)
        p = jax.nn.softmax(s, axis=-1).astype(v.dtype)
        return jnp.einsum('bhts,bshd->bthd', p, v)

    out = lax.map(block, jnp.arange(S // Q_BLOCK))
    return out.transpose(1, 0, 2, 3, 4).reshape(B, S, MLA_HEADS * V_DIM) @ w_out


def clamped_swiglu(gu):
    x_glu = jnp.minimum(gu[..., ::2], SWIGLU_LIMIT)
    x_lin = jnp.clip(gu[..., 1::2], -SWIGLU_LIMIT, SWIGLU_LIMIT)
    return x_glu * jax.nn.sigmoid(SWIGLU_ALPHA * x_glu) * (x_lin + 1.0)


def moe(h, w_r, b_r, w_gu, b_gu, w_dn, b_dn):
    B, S, D = h.shape
    t = h.reshape(B * S, D)
    logits = (t @ w_r + b_r).astype(jnp.float32)
    top_val, top_idx = lax.top_k(logits, TOP_K)
    gates = jax.nn.softmax(top_val, axis=-1)
    combine = jnp.einsum('tk,tke->te', gates,
                         jax.nn.one_hot(top_idx, N_EXPERTS, dtype=jnp.float32)).astype(t.dtype)
    y = jnp.zeros_like(t)
    for e in range(N_EXPERTS):
        act = clamped_swiglu(t @ w_gu[e] + b_gu[e])
        y = y + combine[:, e:e + 1] * (act @ w_dn[e] + b_dn[e])
    return y.reshape(B, S, D)


def setup_inputs(seed: int = 0) -> dict:
    key = jax.random.key(seed)
    ks = jax.random.split(key, 24)
    D = D_MODEL

    def nrm(k, shape, scale):
        return jax.random.normal(k, shape, jnp.float32) * scale

    return {
        'x': nrm(ks[0], (BATCH, SEQ, D), 1.0),
        'c': nrm(ks[1], (BATCH, D), 1.0),
        'w_mod': nrm(ks[2], (DEPTH, D, N_MOD * D), 0.2 * D ** -0.5),
        'b_mod': nrm(ks[3], (DEPTH, N_MOD * D), 0.01),
        'ln1_g': 1.0 + nrm(ks[4], (DEPTH, D), 0.01),
        'ln1_b': nrm(ks[5], (DEPTH, D), 0.01),
        'ln2_g': 1.0 + nrm(ks[6], (DEPTH, D), 0.01),
        'ln2_b': nrm(ks[7], (DEPTH, D), 0.01),
        'even_w_in': nrm(ks[8], (N_EVEN, D, EVEN_IN), D ** -0.5),
        'even_w_out': nrm(ks[9], (N_EVEN, EVEN_OUT, D), DEEPNORM_BETA * EVEN_OUT ** -0.5),
        'idx_ln_g': 1.0 + nrm(ks[10], (N_EVEN, IDX_DIM), 0.01),
        'idx_ln_b': nrm(ks[11], (N_EVEN, IDX_DIM), 0.01),
        'mla_w_down': nrm(ks[12], (N_ODD, D, MLA_DOWN), D ** -0.5),
        'mla_q_norm': 1.0 + nrm(ks[13], (N_ODD, Q_LORA), 0.01),
        'mla_kv_norm': 1.0 + nrm(ks[14], (N_ODD, KV_LORA), 0.01),
        'mla_w_q_up': nrm(ks[15], (N_ODD, Q_LORA, MLA_HEADS * (QK_NOPE + QK_ROPE)), Q_LORA ** -0.5),
        'mla_w_kv_up': nrm(ks[16], (N_ODD, KV_LORA, MLA_HEADS * (QK_NOPE + V_DIM)), KV_LORA ** -0.5),
        'mla_w_out': nrm(ks[17], (N_ODD, MLA_HEADS * V_DIM, D), DEEPNORM_BETA * (MLA_HEADS * V_DIM) ** -0.5),
        'router_w': nrm(ks[18], (DEPTH, D, N_EXPERTS), D ** -0.5),
        'router_b': nrm(ks[19], (DEPTH, N_EXPERTS), 0.01),
        'exp_w_gu': nrm(ks[20], (DEPTH, N_EXPERTS, D, 2 * EXPERT_DIM), D ** -0.5),
        'exp_b_gu': nrm(ks[21], (DEPTH, N_EXPERTS, 2 * EXPERT_DIM), 0.01),
        'exp_w_down': nrm(ks[22], (DEPTH, N_EXPERTS, EXPERT_DIM, D), DEEPNORM_BETA * EXPERT_DIM ** -0.5),
        'exp_b_down': nrm(ks[23], (DEPTH, N_EXPERTS, D), 0.01),
    }


def reference(x, c, w_mod, b_mod, ln1_g, ln1_b, ln2_g, ln2_b, even_w_in, even_w_out,
              idx_ln_g, idx_ln_b, mla_w_down, mla_q_norm, mla_kv_norm, mla_w_q_up,
              mla_w_kv_up, mla_w_out, router_w, router_b, exp_w_gu, exp_b_gu,
              exp_w_down, exp_b_down):
    S = x.shape[1]
    cos_h, sin_h = rope_tables(S, HEAD_DIM)
    cos_i, sin_i = rope_tables(S, IDX_DIM)
    cos_r, sin_r = rope_tables(S, QK_ROPE)
    c_act = jax.nn.silu(c)
    for l in range(DEPTH):
        mod = c_act @ w_mod[l] + b_mod[l]
        sh1, sc1, g1, sh2, sc2, g2 = [m[:, None, :] for m in jnp.split(mod, N_MOD, axis=-1)]
        h = x * (1.0 + sc1) + sh1
        j = l // 2
        if l % 2 == 0:
            y = even_mixer(h, even_w_in[j], even_w_out[j], idx_ln_g[j], idx_ln_b[j],
                           cos_h, sin_h, cos_i, sin_i)
        else:
            y = mla_mixer(h, mla_w_down[j], mla_q_norm[j], mla_kv_norm[j], mla_w_q_up[j],
                          mla_w_kv_up[j], mla_w_out[j], cos_r, sin_r)
        x = layer_norm(DEEPNORM_ALPHA * x + (1.0 + g1) * y, ln1_g[l], ln1_b[l])
        h = x * (1.0 + sc2) + sh2
        y = moe(h, router_w[l], router_b[l], exp_w_gu[l], exp_b_gu[l], exp_w_down[l], exp_b_down[l])
        x = layer_norm(DEEPNORM_ALPHA * x + (1.0 + g2) * y, ln2_g[l], ln2_b[l])
    return x
```

```python
import functools

import jax
import jax.numpy as jnp
import numpy as np
from jax import lax
from jax.experimental import pallas as pl
from jax.experimental.pallas import tpu as pltpu

DEPTH = 4
HEAD_DIM = 128
ROPE_THETA = 10000.0
DSA_HEADS = 8
MOBA_HEADS = 8
IDX_HEADS = 16
IDX_DIM = 64
DSA_TOPK_MAX = 256
MOBA_BLOCK = 256
MOBA_TOPB_MAX = 3
MLA_HEADS = 16
Q_LORA = 512
KV_LORA = 512
QK_NOPE = 128
QK_ROPE = 64
V_DIM = 128
N_EXPERTS = 32
TOP_K = 4
EXPERT_DIM = 512
SWIGLU_LIMIT = 7.0
SWIGLU_ALPHA = 1.702
N_MOD = 6
DEEPNORM_ALPHA = (2 * DEPTH) ** 0.25
NEG = -1e30
LANES = 128
VMEM_LIMIT = 56 * 1024 * 1024
INT_MIN = -2 ** 31

HIGHEST = lax.Precision.HIGHEST
NT_DIMS = (((1,), (1,)), ((), ()))


def _cparams(*sem):
    return pltpu.CompilerParams(dimension_semantics=sem, vmem_limit_bytes=VMEM_LIMIT)


def _dot_nt(a, b):
    return lax.dot_general(a, b, NT_DIMS, preferred_element_type=jnp.float32)


def _dot(a, b):
    return jnp.dot(a, b, preferred_element_type=jnp.float32)


def _mod_body(c_ref, w_ref, b_ref, o_ref):
    c = c_ref[...]
    c_act = c * (1.0 / (1.0 + jnp.exp(-c)))
    o_ref[0] = jnp.dot(c_act, w_ref[0], precision=HIGHEST, preferred_element_type=jnp.float32) + b_ref[0]


def mod_vectors(c, w_mod, b_mod):
    depth, d, n = w_mod.shape
    bsz = c.shape[0]
    rows = 8
    c8 = jnp.zeros((rows, d), jnp.float32).at[:bsz].set(c)
    tn = 1024 if n % 1024 == 0 else 512
    assert n % tn == 0
    out = pl.pallas_call(
        _mod_body,
        grid=(depth, n // tn),
        in_specs=[pl.BlockSpec((rows, d), lambda l, j: (0, 0)),
                  pl.BlockSpec((1, d, tn), lambda l, j: (l, 0, j)),
                  pl.BlockSpec((1, 1, tn), lambda l, j: (l, 0, j))],
        out_specs=pl.BlockSpec((1, rows, tn), lambda l, j: (l, 0, j)),
        out_shape=jax.ShapeDtypeStruct((depth, rows, n), jnp.float32),
        compiler_params=_cparams("arbitrary", "arbitrary"),
        name="mod_vectors",
    )(c8, w_mod, b_mod.reshape(depth, 1, n))
    return out[:, :bsz]


def rope_tables(seq, dim, scale):
    inv = 1.0 / (ROPE_THETA ** (jnp.arange(0, dim, 2, dtype=jnp.float32) / dim))
    ang = jnp.arange(seq, dtype=jnp.float32)[:, None] * inv[None, :]
    cos, sin = jnp.cos(ang), jnp.sin(ang)
    reps = LANES // dim
    zero = jnp.zeros_like(sin)
    c = jnp.tile(jnp.concatenate([cos, cos], axis=1), (1, reps)) * scale
    sa = jnp.tile(jnp.concatenate([-sin, zero], axis=1), (1, reps)) * scale
    sb = jnp.tile(jnp.concatenate([zero, sin], axis=1), (1, reps)) * scale
    return c, sa, sb


def _rope_slab(x, c, sa, sb, half):
    up = pltpu.roll(x, LANES - half, axis=1)
    if 2 * half == LANES:
        return x * c + up * (sa + sb)
    down = pltpu.roll(x, half, axis=1)
    return x * c + up * sa + down * sb


def _proj_body(kinds, x_ref, sc_ref, sh_ref, w_ref, cq_ref, sq_ref, ck_ref, sk_ref,
               ci_ref, sai_ref, sbi_ref, o_ref, wb_ref):
    j = pl.program_id(0)

    @pl.when(pl.program_id(1) == 0)
    def _():
        wb_ref[...] = w_ref[...].astype(jnp.bfloat16)

    h = (x_ref[...] * (1.0 + sc_ref[0]) + sh_ref[0]).astype(jnp.bfloat16)
    acc = _dot(h, wb_ref[...])
    n_slabs = acc.shape[1] // LANES

    def write(fn):
        for s in range(n_slabs):
            sl = slice(s * LANES, (s + 1) * LANES)
            o_ref[:, sl] = fn(acc[:, sl]).astype(o_ref.dtype)

    for kind in sorted(set(kinds)):
        cols = [t for t, k in enumerate(kinds) if k == kind]
        pred = functools.reduce(jnp.logical_or, [j == t for t in cols])

        @pl.when(pred)
        def _(kind=kind):
            if kind == 0:
                write(lambda a: a)
            elif kind == 1:
                c, s = cq_ref[...], sq_ref[...]
                write(lambda a: _rope_slab(a, c, s, 0.0, LANES // 2))
            elif kind == 2:
                c, s = ck_ref[...], sk_ref[...]
                write(lambda a: _rope_slab(a, c, s, 0.0, LANES // 2))
            else:
                c, sa, sb = ci_ref[...], sai_ref[...], sbi_ref[...]
                write(lambda a: _rope_slab(a, c, sa, sb, IDX_DIM // 2))


def modulated_projection(x2, sc, sh, w, kinds, tabs, seq, tm=512, tn=1024):
    t, d = x2.shape
    n = w.shape[1]
    assert n == tn * len(kinds) and t % tm == 0 and seq % tm == 0
    spb = seq // tm
    cq, sq, ck, sk, ci, sai, sbi = tabs
    tab_spec = pl.BlockSpec((tm, LANES), lambda j, i: (i % spb, 0))
    mod_spec = pl.BlockSpec((1, 1, d), lambda j, i: (i // spb, 0, 0))
    return pl.pallas_call(
        functools.partial(_proj_body, tuple(kinds)),
        grid=(n // tn, t // tm),
        in_specs=[pl.BlockSpec((tm, d), lambda j, i: (i, 0)), mod_spec, mod_spec,
                  pl.BlockSpec((d, tn), lambda j, i: (0, j))] + [tab_spec] * 7,
        out_specs=pl.BlockSpec((tm, tn), lambda j, i: (i, j)),
        out_shape=jax.ShapeDtypeStruct((t, n), jnp.bfloat16),
        scratch_shapes=[pltpu.VMEM((d, tn), jnp.bfloat16)],
        compiler_params=_cparams("arbitrary", "arbitrary"),
        name="modulated_projection",
    )(x2, sc, sh, w, cq, sq, ck, sk, ci, sai, sbi)


EVEN_KINDS = (1, 2, 0, 3, 1, 2, 0)


def even_tables(seq):
    cq, saq, sbq = rope_tables(seq, HEAD_DIM, HEAD_DIM ** -0.5)
    ck, sak, sbk = rope_tables(seq, HEAD_DIM, 1.0)
    ci, sai, sbi = rope_tables(seq, IDX_DIM, IDX_DIM ** -0.5)
    return cq, saq + sbq, ck, sak + sbk, ci, sai, sbi


def _idx_proj_body(x_ref, sc_ref, sh_ref, w_ref, g_ref, b_ref, c_ref, sa_ref, sb_ref,
                   klo_ref, khi_ref, wo_ref):
    h = x_ref[...] * (1.0 + sc_ref[0]) + sh_ref[0]
    acc = jnp.dot(h, w_ref[...], precision=HIGHEST, preferred_element_type=jnp.float32)
    lane = lax.broadcasted_iota(jnp.int32, (1, LANES), 1)
    is_k = lane < IDX_DIM
    mu = jnp.sum(jnp.where(is_k, acc, 0.0), axis=1, keepdims=True) * (1.0 / IDX_DIM)
    dev = jnp.where(is_k, acc - mu, 0.0)
    var = jnp.sum(dev * dev, axis=1, keepdims=True) * (1.0 / IDX_DIM)
    kn = dev * lax.rsqrt(var + 1e-5) * g_ref[...] + b_ref[...]
    kr = jnp.where(is_k, _rope_slab(kn, c_ref[...], sa_ref[...], sb_ref[...], IDX_DIM // 2), 0.0)
    klo_ref[...] = kr.astype(jnp.bfloat16)
    khi_ref[...] = pltpu.roll(kr, IDX_DIM, axis=1).astype(jnp.bfloat16)
    wo_ref[...] = acc * (IDX_HEADS ** -0.5)


def idx_projection(x2, sc, sh, w_i, ln_g, ln_b, tabs, seq, tm=512):
    t, d = x2.shape
    spb = seq // tm
    row_spec = pl.BlockSpec((tm, LANES), lambda i: (i, 0))
    tab_spec = pl.BlockSpec((tm, LANES), lambda i: (i % spb, 0))
    mod_spec = pl.BlockSpec((1, 1, d), lambda i: (i // spb, 0, 0))
    vec_spec = pl.BlockSpec((1, LANES), lambda i: (0, 0))
    return pl.pallas_call(
        _idx_proj_body,
        grid=(t // tm,),
        in_specs=[pl.BlockSpec((tm, d), lambda i: (i, 0)), mod_spec, mod_spec,
                  pl.BlockSpec((d, LANES), lambda i: (0, 0)), vec_spec, vec_spec,
                  tab_spec, tab_spec, tab_spec],
        out_specs=[row_spec, row_spec, row_spec],
        out_shape=[jax.ShapeDtypeStruct((t, LANES), jnp.bfloat16),
                   jax.ShapeDtypeStruct((t, LANES), jnp.bfloat16),
                   jax.ShapeDtypeStruct((t, LANES), jnp.float32)],
        compiler_params=_cparams("arbitrary"),
        name="idx_projection",
    )(x2, sc, sh, w_i, ln_g, ln_b, *tabs)


def _flash_step(s, v, carry):
    m, l, acc = carry
    m_new = jnp.maximum(m, jnp.max(s, axis=1, keepdims=True))
    alpha = jnp.exp(m - m_new)
    p = jnp.exp(s - m_new)
    l = alpha * l + jnp.sum(p, axis=1, keepdims=True)
    acc = alpha * acc + _dot(p.astype(jnp.bfloat16), v)
    return m_new, l, acc


def _flash_init(tq, dv):
    return (jnp.full((tq, 1), NEG, jnp.float32), jnp.zeros((tq, 1), jnp.float32),
            jnp.zeros((tq, dv), jnp.float32))


def _dsa_body(topk, kc, qa_ref, qi_ref, w_ref, ka_ref, va_ref, klo_ref, khi_ref, o_ref, key_ref):
    i = pl.program_id(1)
    tq = qa_ref.shape[0]
    t0 = i * tq
    nkc = (t0 + tq + kc - 1) // kc
    row = t0 + lax.broadcasted_iota(jnp.int32, (tq, 1), 0)
    w = w_ref[...]

    def score_chunk(c, _):
        k0 = pl.multiple_of(c * kc, kc)
        klo = klo_ref[pl.ds(k0, kc), :]
        khi = khi_ref[pl.ds(k0, kc), :]
        sc = jnp.zeros((tq, kc), jnp.float32)
        for j in range(IDX_HEADS // 2):
            qj = qi_ref[:, j * LANES:(j + 1) * LANES]
            w0 = w[:, IDX_DIM + 2 * j:IDX_DIM + 2 * j + 1]
            w1 = w[:, IDX_DIM + 2 * j + 1:IDX_DIM + 2 * j + 2]
            sc = sc + w0 * jnp.maximum(_dot_nt(qj, klo), 0.0) + w1 * jnp.maximum(_dot_nt(qj, khi), 0.0)
        col = k0 + lax.broadcasted_iota(jnp.int32, (1, kc), 1)
        bits = pltpu.bitcast(sc, jnp.int32)
        key = bits ^ ((bits >> 31) & 0x7FFFFFFF)
        key_ref[:, pl.ds(k0, kc)] = jnp.where(col <= row, key, INT_MIN)
        return 0

    lax.fori_loop(0, nkc, score_chunk, 0)

    def count_ge(cand):
        def body(c, acc):
            k0 = pl.multiple_of(c * kc, kc)
            ge = jnp.where(key_ref[:, pl.ds(k0, kc)] >= cand, 1, 0)
            for j in range(kc // LANES):
                acc = acc + ge[:, j * LANES:(j + 1) * LANES]
            return acc
        acc = lax.fori_loop(0, nkc, body, jnp.zeros((tq, LANES), jnp.int32))
        return jnp.sum(acc, axis=1, keepdims=True)

    def bit_step(b, thr_u):
        cand_u = thr_u | jnp.left_shift(jnp.int32(1), 31 - b)
        cnt = count_ge(cand_u ^ INT_MIN)
        return jnp.where(cnt >= topk, cand_u, thr_u)

    thr_u = lax.fori_loop(0, 32, bit_step, jnp.zeros((tq, 1), jnp.int32))
    thr = jnp.maximum(thr_u ^ INT_MIN, INT_MIN + 1)

    for h in range(DSA_HEADS):
        hs = slice(h * HEAD_DIM, (h + 1) * HEAD_DIM)
        qh = qa_ref[:, hs]

        def attend(c, carry, hs=hs, qh=qh):
            k0 = pl.multiple_of(c * kc, kc)
            s = _dot_nt(qh, ka_ref[pl.ds(k0, kc), hs])
            s = jnp.where(key_ref[:, pl.ds(k0, kc)] >= thr, s, NEG)
            return _flash_step(s, va_ref[pl.ds(k0, kc), hs], carry)

        _, l, acc = lax.fori_loop(0, nkc, attend, _flash_init(tq, HEAD_DIM))
        o_ref[:, hs] = (acc / l).astype(o_ref.dtype)


def dsa_attention(proj, klo, khi, w16, bsz, seq, tq=256, kc=512):
    t = proj.shape[0]
    width = DSA_HEADS * HEAD_DIM
    topk = min(DSA_TOPK_MAX, seq // 4)
    kc = min(kc, seq)
    assert seq % tq == 0 and seq % kc == 0 and kc >= topk and kc % LANES == 0
    nq = seq // tq
    once = dict(pipeline_mode=pl.Buffered(1))
    return pl.pallas_call(
        functools.partial(_dsa_body, topk, kc),
        grid=(bsz, nq),
        in_specs=[pl.BlockSpec((tq, width), lambda b, i: (b * nq + i, 0)),
                  pl.BlockSpec((tq, width), lambda b, i: (b * nq + i, 3)),
                  pl.BlockSpec((tq, LANES), lambda b, i: (b * nq + i, 0)),
                  pl.BlockSpec((seq, width), lambda b, i: (b, 1), **once),
                  pl.BlockSpec((seq, width), lambda b, i: (b, 2), **once),
                  pl.BlockSpec((seq, LANES), lambda b, i: (b, 0), **once),
                  pl.BlockSpec((seq, LANES), lambda b, i: (b, 0), **once)],
        out_specs=pl.BlockSpec((tq, width), lambda b, i: (b * nq + i, 0)),
        out_shape=jax.ShapeDtypeStruct((t, width), jnp.bfloat16),
        scratch_shapes=[pltpu.VMEM((tq, seq), jnp.int32)],
        compiler_params=_cparams("arbitrary", "arbitrary"),
        name="dsa_attention",
    )(proj, proj, w16, proj, proj, klo, khi)


def _moba_body(nb, topb, q_ref, k_ref, v_ref, o_ref, kmean_ref, bias_ref):
    i = pl.program_id(2)
    tq = q_ref.shape[0]
    blk = MOBA_BLOCK

    @pl.when(i == 0)
    def _():
        kmean_ref[...] = jnp.zeros_like(kmean_ref)

        def mean_block(n, _):
            k0 = pl.multiple_of(n * blk, blk)
            kb = k_ref[pl.ds(k0, blk), :].astype(jnp.float32)
            kmean_ref[pl.ds(n, 1), :] = jnp.sum(kb, axis=0, keepdims=True) * (1.0 / blk)
            return 0
        lax.fori_loop(0, nb, mean_block, 0)

    q = q_ref[...]
    own = i
    nbp = kmean_ref.shape[0]
    gate = lax.dot_general(q.astype(jnp.float32), kmean_ref[...], NT_DIMS, precision=HIGHEST,
                           preferred_element_type=jnp.float32)
    lane = lax.broadcasted_iota(jnp.int32, (1, nbp), 1)
    valid = lane < own
    g = jnp.where(valid, gate, -jnp.inf)
    rank = jnp.zeros((tq, nbp), jnp.int32)
    for m in range(nb):
        gm = g[:, m:m + 1]
        beats = (gm > g) | ((gm == g) & (m < lane))
        rank = rank + jnp.where(beats, 1, 0)
    bias = jnp.where(valid & (rank < topb), 0.0, NEG)
    for n in range(nb):
        bias_ref[n] = jnp.broadcast_to(bias[:, n:n + 1], (tq, LANES))

    k0 = pl.multiple_of(own * blk, blk)
    s = _dot_nt(q, k_ref[pl.ds(k0, blk), :])
    r = lax.broadcasted_iota(jnp.int32, (tq, blk), 0)
    cidx = lax.broadcasted_iota(jnp.int32, (tq, blk), 1)
    s = jnp.where(cidx <= r, s, NEG)
    carry = _flash_step(s, v_ref[pl.ds(k0, blk), :], _flash_init(tq, HEAD_DIM))

    def attend(c, carry):
        kk = pl.multiple_of(c * blk, blk)
        b = bias_ref[c]
        s = _dot_nt(q, k_ref[pl.ds(kk, blk), :]) + jnp.concatenate([b] * (blk // LANES), axis=1)
        return _flash_step(s, v_ref[pl.ds(kk, blk), :], carry)

    _, l, acc = lax.fori_loop(0, own, attend, carry)
    o_ref[...] = (acc / l).astype(o_ref.dtype)


def moba_attention(proj, bsz, seq, col0):
    t = proj.shape[0]
    tq = MOBA_BLOCK
    assert seq % MOBA_BLOCK == 0
    nb = seq // MOBA_BLOCK
    topb = min(MOBA_TOPB_MAX, nb)
    nq = seq // tq
    nbp = max(8, nb)
    hpt = MOBA_HEADS
    return pl.pallas_call(
        functools.partial(_moba_body, nb, topb),
        grid=(bsz, MOBA_HEADS, nq),
        in_specs=[pl.BlockSpec((tq, HEAD_DIM), lambda b, h, i: (b * nq + i, col0 * hpt + h)),
                  pl.BlockSpec((seq, HEAD_DIM), lambda b, h, i: (b, (col0 + 1) * hpt + h)),
                  pl.BlockSpec((seq, HEAD_DIM), lambda b, h, i: (b, (col0 + 2) * hpt + h))],
        out_specs=pl.BlockSpec((tq, HEAD_DIM), lambda b, h, i: (b * nq + i, h)),
        out_shape=jax.ShapeDtypeStruct((t, MOBA_HEADS * HEAD_DIM), jnp.bfloat16),
        scratch_shapes=[pltpu.VMEM((nbp, HEAD_DIM), jnp.float32),
                        pltpu.VMEM((nb, tq, LANES), jnp.float32)],
        compiler_params=_cparams("arbitrary", "arbitrary", "arbitrary"),
        name="moba_attention",
    )(proj, proj, proj)


def _layer_norm_rows(z, g, b):
    mu = jnp.mean(z, axis=1, keepdims=True)
    dev = z - mu
    var = jnp.mean(dev * dev, axis=1, keepdims=True)
    return dev * lax.rsqrt(var + 1e-5) * g + b


def _outproj_body(a1_ref, a2_ref, w_ref, x_ref, gate_ref, lg_ref, lb_ref, o_ref, wb_ref):
    @pl.when(pl.program_id(0) == 0)
    def _():
        wb_ref[...] = w_ref[...].astype(jnp.bfloat16)

    k1 = a1_ref.shape[1]
    y = _dot(a1_ref[...], wb_ref[:k1, :]) + _dot(a2_ref[...], wb_ref[k1:, :])
    z = DEEPNORM_ALPHA * x_ref[...] + (1.0 + gate_ref[0]) * y
    o_ref[...] = _layer_norm_rows(z, lg_ref[...], lb_ref[...])


def outproj_norm(a1, c1, a2, c2, w, x2, gate, ln_g, ln_b, seq, tm=256):
    t, d = x2.shape
    kdim = w.shape[0]
    kh = kdim // 2
    spb = seq // tm
    vec = pl.BlockSpec((1, d), lambda i: (0, 0))
    return pl.pallas_call(
        _outproj_body,
        grid=(t // tm,),
        in_specs=[pl.BlockSpec((tm, kh), lambda i: (i, c1)),
                  pl.BlockSpec((tm, kh), lambda i: (i, c2)),
                  pl.BlockSpec((kdim, d), lambda i: (0, 0), pipeline_mode=pl.Buffered(1)),
                  pl.BlockSpec((tm, d), lambda i: (i, 0)),
                  pl.BlockSpec((1, 1, d), lambda i: (i // spb, 0, 0)), vec, vec],
        out_specs=pl.BlockSpec((tm, d), lambda i: (i, 0)),
        out_shape=jax.ShapeDtypeStruct((t, d), jnp.float32),
        scratch_shapes=[pltpu.VMEM((kdim, d), jnp.bfloat16)],
        compiler_params=_cparams("arbitrary"),
        name="outproj_norm",
    )(a1, a2, w, x2, gate, ln_g, ln_b)


def _rms_rows(v, g):
    return v * lax.rsqrt(jnp.mean(v * v, axis=1, keepdims=True) + 1e-6) * g


def _mla_down_body(x_ref, sc_ref, sh_ref, w_ref, qg_ref, kvg_ref, c_ref, sa_ref, sb_ref,
                   cq_ref, ckv_ref, kr_ref, wb_ref):
    @pl.when(pl.program_id(0) == 0)
    def _():
        wb_ref[...] = w_ref[...].astype(jnp.bfloat16)

    h = (x_ref[...] * (1.0 + sc_ref[0]) + sh_ref[0]).astype(jnp.bfloat16)
    acc = _dot(h, wb_ref[...])
    cq_ref[...] = _rms_rows(acc[:, :Q_LORA], qg_ref[...]).astype(jnp.bfloat16)
    ckv_ref[...] = _rms_rows(acc[:, Q_LORA:Q_LORA + KV_LORA], kvg_ref[...]).astype(jnp.bfloat16)
    lane = lax.broadcasted_iota(jnp.int32, (1, LANES), 1)
    kr = _rope_slab(acc[:, Q_LORA + KV_LORA:], c_ref[...], sa_ref[...], sb_ref[...], QK_ROPE // 2)
    kr = jnp.where(lane < QK_ROPE, kr, 0.0)
    kr_ref[0] = kr.astype(jnp.bfloat16)
    kr_ref[1] = pltpu.roll(kr, QK_ROPE, axis=1).astype(jnp.bfloat16)


def mla_down(x2, sc, sh, w_pad, q_g, kv_g, tabs, seq, tm=512):
    t, d = x2.shape
    n = w_pad.shape[1]
    spb = seq // tm
    tab_spec = pl.BlockSpec((tm, LANES), lambda i: (i % spb, 0))
    mod_spec = pl.BlockSpec((1, 1, d), lambda i: (i // spb, 0, 0))
    return pl.pallas_call(
        _mla_down_body,
        grid=(t // tm,),
        in_specs=[pl.BlockSpec((tm, d), lambda i: (i, 0)), mod_spec, mod_spec,
                  pl.BlockSpec((d, n), lambda i: (0, 0), pipeline_mode=pl.Buffered(1)),
                  pl.BlockSpec((1, Q_LORA), lambda i: (0, 0)),
                  pl.BlockSpec((1, KV_LORA), lambda i: (0, 0)),
                  tab_spec, tab_spec, tab_spec],
        out_specs=[pl.BlockSpec((tm, Q_LORA), lambda i: (i, 0)),
                   pl.BlockSpec((tm, KV_LORA), lambda i: (i, 0)),
                   pl.BlockSpec((2, tm, LANES), lambda i: (0, i, 0))],
        out_shape=[jax.ShapeDtypeStruct((t, Q_LORA), jnp.bfloat16),
                   jax.ShapeDtypeStruct((t, KV_LORA), jnp.bfloat16),
                   jax.ShapeDtypeStruct((2, t, LANES), jnp.bfloat16)],
        scratch_shapes=[pltpu.VMEM((d, n), jnp.bfloat16)],
        compiler_params=_cparams("arbitrary"),
        name="mla_down",
    )(x2, sc, sh, w_pad, q_g, kv_g, *tabs)


def _up_body(kinds, scale, a_ref, w_ref, c_ref, sa_ref, sb_ref, o_ref, wb_ref):
    j = pl.program_id(0)

    @pl.when(pl.program_id(1) == 0)
    def _():
        wb_ref[...] = w_ref[...].astype(jnp.bfloat16)

    acc = _dot(a_ref[...], wb_ref[...])
    n_slabs = acc.shape[1] // LANES

    def write(fn):
        for s in range(n_slabs):
            sl = slice(s * LANES, (s + 1) * LANES)
            o_ref[:, sl] = fn(acc[:, sl]).astype(o_ref.dtype)

    for kind in sorted(set(kinds)):
        cols = [t for t, k in enumerate(kinds) if k == kind]
        pred = functools.reduce(jnp.logical_or, [j == t for t in cols])

        @pl.when(pred)
        def _(kind=kind):
            if kind == 0:
                write(lambda a: a)
            elif kind == 4:
                write(lambda a: a * scale)
            else:
                c, sa, sb = c_ref[...], sa_ref[...], sb_ref[...]
                write(lambda a: _rope_slab(a, c, sa, sb, QK_ROPE // 2))


def up_projection(a, w, kinds, scale, tabs, seq, tm=512, tn=1024):
    t, kdim = a.shape
    n = w.shape[1]
    assert n == tn * len(kinds)
    spb = seq // tm
    tab_spec = pl.BlockSpec((tm, LANES), lambda j, i: (i % spb, 0))
    return pl.pallas_call(
        functools.partial(_up_body, tuple(kinds), scale),
        grid=(n // tn, t // tm),
        in_specs=[pl.BlockSpec((tm, kdim), lambda j, i: (i, 0)),
                  pl.BlockSpec((kdim, tn), lambda j, i: (0, j)), tab_spec, tab_spec, tab_spec],
        out_specs=pl.BlockSpec((tm, tn), lambda j, i: (i, j)),
        out_shape=jax.ShapeDtypeStruct((t, n), jnp.bfloat16),
        scratch_shapes=[pltpu.VMEM((kdim, tn), jnp.bfloat16)],
        compiler_params=_cparams("arbitrary", "arbitrary"),
        name="up_projection",
    )(a, w, *tabs)


def _mla_attn_body(qn_ref, qr_ref, kn_ref, kr_ref, v_ref, o_ref, kcat_ref):
    i = pl.program_id(2)
    tq = qn_ref.shape[0]

    @pl.when(i == 0)
    def _():
        kcat_ref[:, :QK_NOPE] = kn_ref[...]
        kcat_ref[:, QK_NOPE:] = kr_ref[0]

    q = jnp.concatenate([qn_ref[...], qr_ref[...]], axis=1)
    k0 = pl.multiple_of(i * tq, tq)
    s = _dot_nt(q, kcat_ref[pl.ds(k0, tq), :])
    r = lax.broadcasted_iota(jnp.int32, (tq, tq), 0)
    cidx = lax.broadcasted_iota(jnp.int32, (tq, tq), 1)
    s = jnp.where(cidx <= r, s, NEG)
    carry = _flash_step(s, v_ref[pl.ds(k0, tq), :], _flash_init(tq, V_DIM))

    def attend(c, carry):
        kk = pl.multiple_of(c * tq, tq)
        s = _dot_nt(q, kcat_ref[pl.ds(kk, tq), :])
        return _flash_step(s, v_ref[pl.ds(kk, tq), :], carry)

    _, l, acc = lax.fori_loop(0, i, attend, carry)
    o_ref[...] = (acc / l).astype(o_ref.dtype)


def mla_attention(qarr, kv, kr2, bsz, seq, tq=256):
    t = qarr.shape[0]
    nq = seq // tq
    return pl.pallas_call(
        _mla_attn_body,
        grid=(bsz, MLA_HEADS, nq),
        in_specs=[pl.BlockSpec((tq, QK_NOPE), lambda b, h, i: (b * nq + i, h)),
                  pl.BlockSpec((tq, LANES), lambda b, h, i: (b * nq + i, MLA_HEADS + h // 2)),
                  pl.BlockSpec((seq, QK_NOPE), lambda b, h, i: (b, 2 * h)),
                  pl.BlockSpec((1, seq, LANES), lambda b, h, i: (h % 2, b, 0)),
                  pl.BlockSpec((seq, V_DIM), lambda b, h, i: (b, 2 * h + 1))],
        out_specs=pl.BlockSpec((tq, V_DIM), lambda b, h, i: (b * nq + i, h)),
        out_shape=jax.ShapeDtypeStruct((t, MLA_HEADS * V_DIM), jnp.bfloat16),
        scratch_shapes=[pltpu.VMEM((seq, QK_NOPE + LANES), jnp.bfloat16)],
        compiler_params=_cparams("arbitrary", "arbitrary", "arbitrary"),
        name="mla_attention",
    )(qarr, qarr, kv, kr2, kv)


def _pack_halves(v):
    half = v.shape[1] // 2
    bits = pltpu.bitcast(v.astype(jnp.bfloat16).astype(jnp.float32), jnp.int32)
    return lax.shift_right_logical(bits[:, :half], 16) | bits[:, half:]


def _unpack_halves(p):
    lo = pltpu.bitcast(lax.shift_left(p, 16), jnp.float32)
    hi = pltpu.bitcast(p & -65536, jnp.float32)
    return lo, hi


def _router_body(x_ref, sc_ref, sh_ref, wr_ref, br_ref, hp_ref, idx_ref, gate_ref, rank_ref, cnt_ref, run_ref):
    @pl.when(pl.program_id(0) == 0)
    def _():
        run_ref[...] = jnp.zeros_like(run_ref)

    h = x_ref[...] * (1.0 + sc_ref[0]) + sh_ref[0]
    tm = h.shape[0]
    hp_ref[...] = _pack_halves(h)
    logits = jnp.dot(h, wr_ref[...], precision=HIGHEST, preferred_element_type=jnp.float32) + br_ref[...]
    lane = lax.broadcasted_iota(jnp.int32, (1, LANES), 1).astype(jnp.float32)
    vals, idxs, hots = [], [], []
    rest = logits
    for _ in range(TOP_K):
        m = jnp.max(rest, axis=1, keepdims=True)
        idx = jnp.min(jnp.where(rest == m, lane, float(LANES)), axis=1, keepdims=True)
        hot = lane == idx
        vals.append(m)
        idxs.append(idx)
        hots.append(hot)
        rest = jnp.where(hot, -jnp.inf, rest)
    exps = [jnp.exp(v - vals[0]) for v in vals]
    den = functools.reduce(jnp.add, exps)
    hot_all = functools.reduce(jnp.add, [jnp.where(hot, 1.0, 0.0) for hot in hots])
    r = lax.broadcasted_iota(jnp.int32, (tm, tm), 0)
    c = lax.broadcasted_iota(jnp.int32, (tm, tm), 1)
    below = jnp.where(c < r, 1.0, 0.0).astype(jnp.bfloat16)
    before = _dot(below, hot_all.astype(jnp.bfloat16)) + run_ref[0:1, :]
    idx_out = jnp.zeros((tm, LANES), jnp.float32)
    gate_out = jnp.zeros((tm, LANES), jnp.float32)
    rank_out = jnp.zeros((tm, LANES), jnp.float32)
    for k in range(TOP_K):
        slot = lane == float(k)
        rank_k = jnp.sum(jnp.where(hots[k], before, 0.0), axis=1, keepdims=True)
        idx_out = jnp.where(slot, idxs[k], idx_out)
        gate_out = jnp.where(slot, exps[k] / den, gate_out)
        rank_out = jnp.where(slot, rank_k, rank_out)
    idx_ref[...] = idx_out.astype(jnp.int32)
    gate_ref[...] = gate_out
    rank_ref[...] = rank_out.astype(jnp.int32)
    run_ref[...] = run_ref[...] + jnp.sum(hot_all, axis=0, keepdims=True)
    cnt_ref[...] = run_ref[...].astype(jnp.int32)


def moe_router(x2, sc, sh, w_r, b_r, seq, tm=512):
    t, d = x2.shape
    e = w_r.shape[1]
    spb = seq // tm
    wr = jnp.pad(w_r, ((0, 0), (0, LANES - e)))
    br = jnp.pad(b_r, (0, LANES - e), constant_values=NEG)[None]
    mod_spec = pl.BlockSpec((1, 1, d), lambda i: (i // spb, 0, 0))
    row = pl.BlockSpec((tm, LANES), lambda i: (i, 0))
    return pl.pallas_call(
        _router_body,
        grid=(t // tm,),
        in_specs=[pl.BlockSpec((tm, d), lambda i: (i, 0)), mod_spec, mod_spec,
                  pl.BlockSpec((d, LANES), lambda i: (0, 0)),
                  pl.BlockSpec((1, LANES), lambda i: (0, 0))],
        out_specs=[pl.BlockSpec((tm, d // 2), lambda i: (i, 0)), row, row, row,
                   pl.BlockSpec((8, LANES), lambda i: (0, 0))],
        out_shape=[jax.ShapeDtypeStruct((t, d // 2), jnp.int32),
                   jax.ShapeDtypeStruct((t, LANES), jnp.int32),
                   jax.ShapeDtypeStruct((t, LANES), jnp.float32),
                   jax.ShapeDtypeStruct((t, LANES), jnp.int32),
                   jax.ShapeDtypeStruct((8, LANES), jnp.int32)],
        scratch_shapes=[pltpu.VMEM((8, LANES), jnp.float32)],
        compiler_params=_cparams("arbitrary"),
        name="moe_router",
    )(x2, sc, sh, wr, br)


def _row_gather_start(src_hbm, dst, sem, rows, row_of):
    def body(r, _):
        pltpu.make_async_copy(src_hbm.at[pl.ds(row_of(r), 1), :], dst.at[pl.ds(r, 1), :], sem).start()
        return 0
    lax.fori_loop(0, rows, body, 0)


def _row_gather_wait(src_hbm, dst, sem, rows):
    pltpu.make_async_copy(src_hbm.at[pl.ds(0, rows), :], dst, sem).wait()


def _experts_body(te_ref, src_ref, nu_ref, h_hbm, wgu_ref, bgu_ref, wdn_ref, bdn_ref, o_ref,
                  xbuf, sem, wdn_b):
    j = pl.program_id(0)
    nused = nu_ref[0]
    tmg = xbuf.shape[1]

    def issue(tile, slot):
        _row_gather_start(h_hbm, xbuf.at[slot], sem.at[slot], tmg, lambda r: src_ref[tile * tmg + r])

    @pl.when(j == 0)
    def _():
        issue(0, 0)

    @pl.when(j + 1 < nused)
    def _():
        issue(j + 1, (j + 1) % 2)

    @pl.when(j < nused)
    def _():
        slot = j % 2
        _row_gather_wait(h_hbm, xbuf.at[slot], sem.at[slot], tmg)

        @pl.when((j == 0) | (te_ref[j] != te_ref[jnp.maximum(j - 1, 0)]))
        def _():
            wdn_b[...] = wdn_ref[0].astype(jnp.bfloat16)

        lo, hi = _unpack_halves(xbuf[slot])
        half = lo.shape[1]
        gu = (_dot(lo.astype(jnp.bfloat16), wgu_ref[0, :half, :])
              + _dot(hi.astype(jnp.bfloat16), wgu_ref[0, half:, :]) + bgu_ref[0])
        glu = jnp.minimum(gu[:, :EXPERT_DIM], SWIGLU_LIMIT)
        lin = jnp.clip(gu[:, EXPERT_DIM:], -SWIGLU_LIMIT, SWIGLU_LIMIT)
        act = glu * jax.nn.sigmoid(SWIGLU_ALPHA * glu) * (lin + 1.0)
        y = _dot(act.astype(jnp.bfloat16), wdn_b[...]) + bdn_ref[0]
        o_ref[...] = _pack_halves(y)

    @pl.when(j >= nused)
    def _():
        o_ref[...] = jnp.zeros_like(o_ref)


def moe_experts(hp, tile_expert, row_src, nused, wgu, bgu, wdn, bdn, tmg):
    t, half = hp.shape
    d = 2 * half
    nt = tile_expert.shape[0]
    e, _, f2 = wgu.shape
    grid_spec = pltpu.PrefetchScalarGridSpec(
        num_scalar_prefetch=3,
        grid=(nt,),
        in_specs=[pl.BlockSpec(memory_space=pl.ANY),
                  pl.BlockSpec((1, d, f2), lambda j, te, src, nu: (te[j], 0, 0)),
                  pl.BlockSpec((1, 1, f2), lambda j, te, src, nu: (te[j], 0, 0)),
                  pl.BlockSpec((1, f2 // 2, d), lambda j, te, src, nu: (te[j], 0, 0)),
                  pl.BlockSpec((1, 1, d), lambda j, te, src, nu: (te[j], 0, 0))],
        out_specs=pl.BlockSpec((tmg, half), lambda j, te, src, nu: (j, 0)),
        scratch_shapes=[pltpu.VMEM((2, tmg, half), jnp.int32),
                        pltpu.SemaphoreType.DMA((2,)),
                        pltpu.VMEM((f2 // 2, d), jnp.bfloat16)])
    return pl.pallas_call(
        _experts_body,
        grid_spec=grid_spec,
        out_shape=jax.ShapeDtypeStruct((nt * tmg, half), jnp.int32),
        compiler_params=_cparams("arbitrary"),
        name="moe_experts",
    )(tile_expert, row_src, nused, hp, wgu, bgu, wdn, bdn)


def _combine_body(dest_ref, og_hbm, gate_ref, x_ref, g2_ref, lg_ref, lb_ref, o_ref, buf, sem):
    i = pl.program_id(0)
    tmc = x_ref.shape[0]

    def issue(tile, slot):
        for k in range(TOP_K):
            _row_gather_start(og_hbm, buf.at[slot, k], sem.at[slot], tmc,
                              lambda r, k=k: dest_ref[(tile * tmc + r) * TOP_K + k])

    @pl.when(i == 0)
    def _():
        issue(0, 0)

    @pl.when(i + 1 < pl.num_programs(0))
    def _():
        issue(i + 1, (i + 1) % 2)

    slot = i % 2
    for k in range(TOP_K):
        _row_gather_wait(og_hbm, buf.at[slot, k], sem.at[slot], tmc)
    gate = gate_ref[...]
    y_lo = y_hi = None
    for k in range(TOP_K):
        lo, hi = _unpack_halves(buf[slot, k])
        gk = gate[:, k:k + 1]
        y_lo = gk * lo if y_lo is None else y_lo + gk * lo
        y_hi = gk * hi if y_hi is None else y_hi + gk * hi
    y = jnp.concatenate([y_lo, y_hi], axis=1)
    z = DEEPNORM_ALPHA * x_ref[...] + (1.0 + g2_ref[0]) * y
    o_ref[...] = _layer_norm_rows(z, lg_ref[...], lb_ref[...])


def moe_combine_norm(og, dest_flat, gates, x2, g2, ln_g, ln_b, seq, tmc=256):
    t, d = x2.shape
    spb = seq // tmc
    vec = pl.BlockSpec((1, d), lambda i, dst: (0, 0))
    grid_spec = pltpu.PrefetchScalarGridSpec(
        num_scalar_prefetch=1,
        grid=(t // tmc,),
        in_specs=[pl.BlockSpec(memory_space=pl.ANY),
                  pl.BlockSpec((tmc, LANES), lambda i, dst: (i, 0)),
                  pl.BlockSpec((tmc, d), lambda i, dst: (i, 0)),
                  pl.BlockSpec((1, 1, d), lambda i, dst: (i // spb, 0, 0)), vec, vec],
        out_specs=pl.BlockSpec((tmc, d), lambda i, dst: (i, 0)),
        scratch_shapes=[pltpu.VMEM((2, TOP_K, tmc, d // 2), jnp.int32),
                        pltpu.SemaphoreType.DMA((2,))])
    return pl.pallas_call(
        _combine_body,
        grid_spec=grid_spec,
        out_shape=jax.ShapeDtypeStruct((t, d), jnp.float32),
        compiler_params=_cparams("arbitrary"),
        name="moe_combine_norm",
    )(dest_flat, og, gates, x2, g2, ln_g, ln_b)


def moe_layer(x2, sc, sh, g2, ln_g, ln_b, w_r, b_r, w_gu, b_gu, w_dn, b_dn, seq, tmg=256):
    t, d = x2.shape
    e = w_r.shape[1]
    f = w_dn.shape[1]
    hp, idx, gates, rank, cnt = moe_router(x2, sc, sh, w_r, b_r, seq)
    idx, rank, counts = idx[:, :TOP_K], rank[:, :TOP_K], cnt[0, :e]
    tiles = (counts + tmg - 1) // tmg
    cum = jnp.cumsum(tiles)
    dest = ((cum - tiles)[idx] * tmg + rank).reshape(-1)
    nt = (t * TOP_K) // tmg + e
    tile_expert = jnp.minimum(jnp.searchsorted(cum, jnp.arange(nt, dtype=jnp.int32), side="right"), e - 1)
    row_src = jnp.zeros((nt * tmg,), jnp.int32).at[dest].set(jnp.repeat(jnp.arange(t, dtype=jnp.int32), TOP_K))
    wgu = jnp.concatenate([w_gu[:, :, 0::2], w_gu[:, :, 1::2]], axis=2).astype(jnp.bfloat16)
    bgu = jnp.concatenate([b_gu[:, 0::2], b_gu[:, 1::2]], axis=1)[:, None, :]
    og = moe_experts(hp, tile_expert.astype(jnp.int32), row_src, cum[-1:].astype(jnp.int32),
                     wgu, bgu, w_dn, b_dn[:, None, :], tmg)
    return moe_combine_norm(og, dest.astype(jnp.int32), gates, x2, g2, ln_g, ln_b, seq)


def mla_mixer_core(x2, sc, sh, w_down, q_norm, kv_norm, w_q_up, w_kv_up, bsz, seq):
    d = x2.shape[1]
    pad = (-w_down.shape[1]) % LANES
    w_pad = jnp.pad(w_down, ((0, 0), (0, pad)))
    k_tabs = rope_tables(seq, QK_ROPE, 1.0)
    scale = (QK_NOPE + QK_ROPE) ** -0.5
    q_tabs = rope_tables(seq, QK_ROPE, scale)
    cq, ckv, kr2 = mla_down(x2, sc, sh, w_pad, q_norm[None], kv_norm[None], k_tabs, seq)
    wq = w_q_up.reshape(Q_LORA, MLA_HEADS, QK_NOPE + QK_ROPE)
    wq = jnp.concatenate([wq[:, :, :QK_NOPE].reshape(Q_LORA, -1), wq[:, :, QK_NOPE:].reshape(Q_LORA, -1)], axis=1)
    qarr = up_projection(cq, wq, (4, 4, 3), scale, q_tabs, seq)
    kv = up_projection(ckv, w_kv_up, (0, 0, 0, 0), scale, q_tabs, seq)
    return mla_attention(qarr, kv, kr2, bsz, seq)


def even_mixer_core(x2, sc, sh, w_in, idx_g, idx_b, tabs, ki_tabs, bsz, seq):
    a_cols = 3 * DSA_HEADS * HEAD_DIM + IDX_HEADS * IDX_DIM
    i_cols = IDX_DIM + IDX_HEADS
    w_main = jnp.concatenate([w_in[:, :a_cols], w_in[:, a_cols + i_cols:]], axis=1)
    w_idx = jnp.pad(w_in[:, a_cols:a_cols + i_cols], ((0, 0), (0, LANES - i_cols)))
    proj = modulated_projection(x2, sc, sh, w_main, EVEN_KINDS, tabs, seq)
    klo, khi, w16 = idx_projection(x2, sc, sh, w_idx, jnp.pad(idx_g, (0, LANES - IDX_DIM))[None],
                                   jnp.pad(idx_b, (0, LANES - IDX_DIM))[None], ki_tabs, seq)
    out_a = dsa_attention(proj, klo, khi, w16, bsz, seq)
    out_b = moba_attention(proj, bsz, seq, 4)
    return out_a, out_b


@jax.jit
def _forward(x, c, w_mod, b_mod, ln1_g, ln1_b, ln2_g, ln2_b, even_w_in, even_w_out, idx_ln_g, idx_ln_b,
             mla_w_down, mla_q_norm, mla_kv_norm, mla_w_q_up, mla_w_kv_up, mla_w_out, router_w, router_b,
             exp_w_gu, exp_b_gu, exp_w_down, exp_b_down):
    bsz, seq, d = x.shape
    depth = w_mod.shape[0]
    mods = mod_vectors(c, w_mod, b_mod).reshape(depth, bsz, N_MOD, 1, d)
    tabs = even_tables(seq)
    ki_tabs = rope_tables(seq, IDX_DIM, 1.0)
    x2 = x.reshape(bsz * seq, d)
    for l in range(depth):
        sh1, sc1, g1, sh2, sc2, g2 = [mods[l, :, m] for m in range(N_MOD)]
        j = l // 2
        if l % 2 == 0:
            out_a, out_b = even_mixer_core(x2, sc1, sh1, even_w_in[j], idx_ln_g[j], idx_ln_b[j],
                                           tabs, ki_tabs, bsz, seq)
            x2 = outproj_norm(out_a, 0, out_b, 0, even_w_out[j], x2, g1, ln1_g[l][None], ln1_b[l][None], seq)
        else:
            attn = mla_mixer_core(x2, sc1, sh1, mla_w_down[j], mla_q_norm[j], mla_kv_norm[j],
                                  mla_w_q_up[j], mla_w_kv_up[j], bsz, seq)
            x2 = outproj_norm(attn, 0, attn, 1, mla_w_out[j], x2, g1, ln1_g[l][None], ln1_b[l][None], seq)
        x2 = moe_layer(x2, sc2, sh2, g2, ln2_g[l][None], ln2_b[l][None], router_w[l], router_b[l],
                       exp_w_gu[l], exp_b_gu[l], exp_w_down[l], exp_b_down[l], seq)
    return x2.reshape(bsz, seq, d)


def kernel(x, c, w_mod, b_mod, ln1_g, ln1_b, ln2_g, ln2_b, even_w_in, even_w_out, idx_ln_g, idx_ln_b,
           mla_w_down, mla_q_norm, mla_kv_norm, mla_w_q_up, mla_w_kv_up, mla_w_out, router_w, router_b,
           exp_w_gu, exp_b_gu, exp_w_down, exp_b_down):
    return _forward(x, c, w_mod, b_mod, ln1_g, ln1_b, ln2_g, ln2_b, even_w_in, even_w_out, idx_ln_g,
                    idx_ln_b, mla_w_down, mla_q_norm, mla_kv_norm, mla_w_q_up, mla_w_kv_up, mla_w_out,
                    router_w, router_b, exp_w_gu, exp_b_gu, exp_w_down, exp_b_down)
```

```python
import functools

import jax
import jax.numpy as jnp
import numpy as np
from jax import lax
from jax.experimental import pallas as pl
from jax.experimental.pallas import tpu as pltpu

DEPTH = 4
HEAD_DIM = 128
ROPE_THETA = 10000.0
DSA_HEADS = 8
MOBA_HEADS = 8
IDX_HEADS = 16
IDX_DIM = 64
DSA_TOPK_MAX = 256
MOBA_BLOCK = 256
MOBA_TOPB_MAX = 3
MLA_HEADS = 16
Q_LORA = 512
KV_LORA = 512
QK_NOPE = 128
QK_ROPE = 64
V_DIM = 128
N_EXPERTS = 32
TOP_K = 4
EXPERT_DIM = 512
SWIGLU_LIMIT = 7.0
SWIGLU_ALPHA = 1.702
N_MOD = 6
DEEPNORM_ALPHA = (2 * DEPTH) ** 0.25
NEG = -1e30
LANES = 128
VMEM_LIMIT = 56 * 1024 * 1024
INT_MIN = -2 ** 31

HIGHEST = lax.Precision.HIGHEST
NT_DIMS = (((1,), (1,)), ((), ()))


def _cparams(*sem):
    return pltpu.CompilerParams(dimension_semantics=sem, vmem_limit_bytes=VMEM_LIMIT)


def _dot_nt(a, b):
    return lax.dot_general(a, b, NT_DIMS, preferred_element_type=jnp.float32)


def _dot(a, b):
    return jnp.dot(a, b, preferred_element_type=jnp.float32)


def _mod_body(c_ref, w_ref, b_ref, o_ref):
    c = c_ref[...]
    c_act = c * (1.0 / (1.0 + jnp.exp(-c)))
    o_ref[0] = jnp.dot(c_act, w_ref[0], precision=HIGHEST, preferred_element_type=jnp.float32) + b_ref[0]


def mod_vectors(c, w_mod, b_mod):
    depth, d, n = w_mod.shape
    bsz = c.shape[0]
    rows = 8
    c8 = jnp.zeros((rows, d), jnp.float32).at[:bsz].set(c)
    tn = 1024 if n % 1024 == 0 else 512
    assert n % tn == 0
    out = pl.pallas_call(
        _mod_body,
        grid=(depth, n // tn),
        in_specs=[pl.BlockSpec((rows, d), lambda l, j: (0, 0)),
                  pl.BlockSpec((1, d, tn), lambda l, j: (l, 0, j)),
                  pl.BlockSpec((1, 1, tn), lambda l, j: (l, 0, j))],
        out_specs=pl.BlockSpec((1, rows, tn), lambda l, j: (l, 0, j)),
        out_shape=jax.ShapeDtypeStruct((depth, rows, n), jnp.float32),
        compiler_params=_cparams("arbitrary", "arbitrary"),
        name="mod_vectors",
    )(c8, w_mod, b_mod.reshape(depth, 1, n))
    return out[:, :bsz]


def rope_tables(seq, dim, scale):
    inv = 1.0 / (ROPE_THETA ** (jnp.arange(0, dim, 2, dtype=jnp.float32) / dim))
    ang = jnp.arange(seq, dtype=jnp.float32)[:, None] * inv[None, :]
    cos, sin = jnp.cos(ang), jnp.sin(ang)
    reps = LANES // dim
    zero = jnp.zeros_like(sin)
    c = jnp.tile(jnp.concatenate([cos, cos], axis=1), (1, reps)) * scale
    sa = jnp.tile(jnp.concatenate([-sin, zero], axis=1), (1, reps)) * scale
    sb = jnp.tile(jnp.concatenate([zero, sin], axis=1), (1, reps)) * scale
    return c, sa, sb


def _rope_slab(x, c, sa, sb, half):
    up = pltpu.roll(x, LANES - half, axis=1)
    if 2 * half == LANES:
        return x * c + up * (sa + sb)
    down = pltpu.roll(x, half, axis=1)
    return x * c + up * sa + down * sb


def _proj_body(kinds, x_ref, sc_ref, sh_ref, w_ref, cq_ref, sq_ref, ck_ref, sk_ref,
               ci_ref, sai_ref, sbi_ref, o_ref, wb_ref):
    j = pl.program_id(0)

    @pl.when(pl.program_id(1) == 0)
    def _():
        wb_ref[...] = w_ref[...].astype(jnp.bfloat16)

    h = (x_ref[...] * (1.0 + sc_ref[0]) + sh_ref[0]).astype(jnp.bfloat16)
    acc = _dot(h, wb_ref[...])
    n_slabs = acc.shape[1] // LANES

    def write(fn):
        for s in range(n_slabs):
            sl = slice(s * LANES, (s + 1) * LANES)
            o_ref[:, sl] = fn(acc[:, sl]).astype(o_ref.dtype)

    for kind in sorted(set(kinds)):
        cols = [t for t, k in enumerate(kinds) if k == kind]
        pred = functools.reduce(jnp.logical_or, [j == t for t in cols])

        @pl.when(pred)
        def _(kind=kind):
            if kind == 0:
                write(lambda a: a)
            elif kind == 1:
                c, s = cq_ref[...], sq_ref[...]
                write(lambda a: _rope_slab(a, c, s, 0.0, LANES // 2))
            elif kind == 2:
                c, s = ck_ref[...], sk_ref[...]
                write(lambda a: _rope_slab(a, c, s, 0.0, LANES // 2))
            else:
                c, sa, sb = ci_ref[...], sai_ref[...], sbi_ref[...]
                write(lambda a: _rope_slab(a, c, sa, sb, IDX_DIM // 2))


def modulated_projection(x2, sc, sh, w, kinds, tabs, seq, tm=512, tn=1024):
    t, d = x2.shape
    n = w.shape[1]
    assert n == tn * len(kinds) and t % tm == 0 and seq % tm == 0
    spb = seq // tm
    cq, sq, ck, sk, ci, sai, sbi = tabs
    tab_spec = pl.BlockSpec((tm, LANES), lambda j, i: (i % spb, 0))
    mod_spec = pl.BlockSpec((1, 1, d), lambda j, i: (i // spb, 0, 0))
    return pl.pallas_call(
        functools.partial(_proj_body, tuple(kinds)),
        grid=(n // tn, t // tm),
        in_specs=[pl.BlockSpec((tm, d), lambda j, i: (i, 0)), mod_spec, mod_spec,
                  pl.BlockSpec((d, tn), lambda j, i: (0, j))] + [tab_spec] * 7,
        out_specs=pl.BlockSpec((tm, tn), lambda j, i: (i, j)),
        out_shape=jax.ShapeDtypeStruct((t, n), jnp.bfloat16),
        scratch_shapes=[pltpu.VMEM((d, tn), jnp.bfloat16)],
        compiler_params=_cparams("arbitrary", "arbitrary"),
        name="modulated_projection",
    )(x2, sc, sh, w, cq, sq, ck, sk, ci, sai, sbi)


EVEN_KINDS = (1, 2, 0, 3, 1, 2, 0)


def even_tables(seq):
    cq, saq, sbq = rope_tables(seq, HEAD_DIM, HEAD_DIM ** -0.5)
    ck, sak, sbk = rope_tables(seq, HEAD_DIM, 1.0)
    ci, sai, sbi = rope_tables(seq, IDX_DIM, IDX_DIM ** -0.5)
    return cq, saq + sbq, ck, sak + sbk, ci, sai, sbi


def _idx_proj_body(x_ref, sc_ref, sh_ref, w_ref, g_ref, b_ref, c_ref, sa_ref, sb_ref,
                   klo_ref, khi_ref, wo_ref):
    h = x_ref[...] * (1.0 + sc_ref[0]) + sh_ref[0]
    acc = jnp.dot(h, w_ref[...], precision=HIGHEST, preferred_element_type=jnp.float32)
    lane = lax.broadcasted_iota(jnp.int32, (1, LANES), 1)
    is_k = lane < IDX_DIM
    mu = jnp.sum(jnp.where(is_k, acc, 0.0), axis=1, keepdims=True) * (1.0 / IDX_DIM)
    dev = jnp.where(is_k, acc - mu, 0.0)
    var = jnp.sum(dev * dev, axis=1, keepdims=True) * (1.0 / IDX_DIM)
    kn = dev * lax.rsqrt(var + 1e-5) * g_ref[...] + b_ref[...]
    kr = jnp.where(is_k, _rope_slab(kn, c_ref[...], sa_ref[...], sb_ref[...], IDX_DIM // 2), 0.0)
    klo_ref[...] = kr.astype(jnp.bfloat16)
    khi_ref[...] = pltpu.roll(kr, IDX_DIM, axis=1).astype(jnp.bfloat16)
    wo_ref[...] = acc * (IDX_HEADS ** -0.5)


def idx_projection(x2, sc, sh, w_i, ln_g, ln_b, tabs, seq, tm=512):
    t, d = x2.shape
    spb = seq // tm
    row_spec = pl.BlockSpec((tm, LANES), lambda i: (i, 0))
    tab_spec = pl.BlockSpec((tm, LANES), lambda i: (i % spb, 0))
    mod_spec = pl.BlockSpec((1, 1, d), lambda i: (i // spb, 0, 0))
    vec_spec = pl.BlockSpec((1, LANES), lambda i: (0, 0))
    return pl.pallas_call(
        _idx_proj_body,
        grid=(t // tm,),
        in_specs=[pl.BlockSpec((tm, d), lambda i: (i, 0)), mod_spec, mod_spec,
                  pl.BlockSpec((d, LANES), lambda i: (0, 0)), vec_spec, vec_spec,
                  tab_spec, tab_spec, tab_spec],
        out_specs=[row_spec, row_spec, row_spec],
        out_shape=[jax.ShapeDtypeStruct((t, LANES), jnp.bfloat16),
                   jax.ShapeDtypeStruct((t, LANES), jnp.bfloat16),
                   jax.ShapeDtypeStruct((t, LANES), jnp.float32)],
        compiler_params=_cparams("arbitrary"),
        name="idx_projection",
    )(x2, sc, sh, w_i, ln_g, ln_b, *tabs)


def _flash_step(s, v, carry):
    m, l, acc = carry
    m_new = jnp.maximum(m, jnp.max(s, axis=1, keepdims=True))
    alpha = jnp.exp(m - m_new)
    p = jnp.exp(s - m_new)
    l = alpha * l + jnp.sum(p, axis=1, keepdims=True)
    acc = alpha * acc + _dot(p.astype(jnp.bfloat16), v)
    return m_new, l, acc


def _flash_init(tq, dv):
    return (jnp.full((tq, 1), NEG, jnp.float32), jnp.zeros((tq, 1), jnp.float32),
            jnp.zeros((tq, dv), jnp.float32))


def _dsa_body(topk, kc, qa_ref, qi_ref, w_ref, ka_ref, va_ref, klo_ref, khi_ref, o_ref, key_ref):
    i = pl.program_id(1)
    tq = qa_ref.shape[0]
    t0 = i * tq
    nkc = (t0 + tq + kc - 1) // kc
    row = t0 + lax.broadcasted_iota(jnp.int32, (tq, 1), 0)
    w = w_ref[...]

    def score_chunk(c, _):
        k0 = pl.multiple_of(c * kc, kc)
        klo = klo_ref[pl.ds(k0, kc), :]
        khi = khi_ref[pl.ds(k0, kc), :]
        sc = jnp.zeros((tq, kc), jnp.float32)
        for j in range(IDX_HEADS // 2):
            qj = qi_ref[:, j * LANES:(j + 1) * LANES]
            w0 = w[:, IDX_DIM + 2 * j:IDX_DIM + 2 * j + 1]
            w1 = w[:, IDX_DIM + 2 * j + 1:IDX_DIM + 2 * j + 2]
            sc = sc + w0 * jnp.maximum(_dot_nt(qj, klo), 0.0) + w1 * jnp.maximum(_dot_nt(qj, khi), 0.0)
        col = k0 + lax.broadcasted_iota(jnp.int32, (1, kc), 1)
        bits = pltpu.bitcast(sc, jnp.int32)
        key = bits ^ ((bits >> 31) & 0x7FFFFFFF)
        key_ref[:, pl.ds(k0, kc)] = jnp.where(col <= row, key, INT_MIN)
        return 0

    lax.fori_loop(0, nkc, score_chunk, 0)

    def count_ge(cand):
        def body(c, acc):
            k0 = pl.multiple_of(c * kc, kc)
            ge = jnp.where(key_ref[:, pl.ds(k0, kc)] >= cand, 1, 0)
            for j in range(kc // LANES):
                acc = acc + ge[:, j * LANES:(j + 1) * LANES]
            return acc
        acc = lax.fori_loop(0, nkc, body, jnp.zeros((tq, LANES), jnp.int32))
        return jnp.sum(acc, axis=1, keepdims=True)

    def bit_step(b, thr_u):
        cand_u = thr_u | jnp.left_shift(jnp.int32(1), 31 - b)
        cnt = count_ge(cand_u ^ INT_MIN)
        return jnp.where(cnt >= topk, cand_u, thr_u)

    thr_u = lax.fori_loop(0, 32, bit_step, jnp.zeros((tq, 1), jnp.int32))
    thr = jnp.maximum(thr_u ^ INT_MIN, INT_MIN + 1)

    group = 4
    for h0 in range(0, DSA_HEADS, group):
        hss = [slice(h * HEAD_DIM, (h + 1) * HEAD_DIM) for h in range(h0, h0 + group)]
        qhs = [qa_ref[:, hs] for hs in hss]

        def attend(c, carry, hss=hss, qhs=qhs):
            k0 = pl.multiple_of(c * kc, kc)
            bias = jnp.where(key_ref[:, pl.ds(k0, kc)] >= thr, 0.0, NEG)
            return tuple(
                _flash_step(_dot_nt(qh, ka_ref[pl.ds(k0, kc), hs]) + bias, va_ref[pl.ds(k0, kc), hs], cr)
                for qh, hs, cr in zip(qhs, hss, carry))

        carry = lax.fori_loop(0, nkc, attend, tuple(_flash_init(tq, HEAD_DIM) for _ in hss))
        for hs, (_, l, acc) in zip(hss, carry):
            o_ref[:, hs] = (acc / l).astype(o_ref.dtype)


def dsa_attention(proj, klo, khi, w16, bsz, seq, tq=256, kc=512):
    t = proj.shape[0]
    width = DSA_HEADS * HEAD_DIM
    topk = min(DSA_TOPK_MAX, seq // 4)
    kc = min(kc, seq)
    assert seq % tq == 0 and seq % kc == 0 and kc >= topk and kc % LANES == 0
    nq = seq // tq
    once = dict(pipeline_mode=pl.Buffered(1))
    return pl.pallas_call(
        functools.partial(_dsa_body, topk, kc),
        grid=(bsz, nq),
        in_specs=[pl.BlockSpec((tq, width), lambda b, i: (b * nq + i, 0)),
                  pl.BlockSpec((tq, width), lambda b, i: (b * nq + i, 3)),
                  pl.BlockSpec((tq, LANES), lambda b, i: (b * nq + i, 0)),
                  pl.BlockSpec((seq, width), lambda b, i: (b, 1), **once),
                  pl.BlockSpec((seq, width), lambda b, i: (b, 2), **once),
                  pl.BlockSpec((seq, LANES), lambda b, i: (b, 0), **once),
                  pl.BlockSpec((seq, LANES), lambda b, i: (b, 0), **once)],
        out_specs=pl.BlockSpec((tq, width), lambda b, i: (b * nq + i, 0)),
        out_shape=jax.ShapeDtypeStruct((t, width), jnp.bfloat16),
        scratch_shapes=[pltpu.VMEM((tq, seq), jnp.int32)],
        compiler_params=_cparams("arbitrary", "arbitrary"),
        name="dsa_attention",
    )(proj, proj, w16, proj, proj, klo, khi)


def _moba_body(nb, topb, q_ref, k_ref, v_ref, o_ref, kmean_ref, kaug_ref):
    i = pl.program_id(2)
    tq = q_ref.shape[0]
    blk = MOBA_BLOCK
    pair = (0, 1)
    nbp = kmean_ref.shape[1]
    t0 = i * tq

    @pl.when(i == 0)
    def _():
        kmean_ref[...] = jnp.zeros_like(kmean_ref)
        lane = lax.broadcasted_iota(jnp.int32, (blk, LANES), 1)

        def per_block(n, _):
            k0 = pl.multiple_of(n * blk, blk)
            hot = jnp.where(lane == n, 1.0, 0.0).astype(jnp.bfloat16)
            for g in pair:
                kb = k_ref[pl.ds(k0, blk), g * HEAD_DIM:(g + 1) * HEAD_DIM]
                kmean_ref[g, pl.ds(n, 1), :] = jnp.sum(kb.astype(jnp.float32), axis=0, keepdims=True) * (1.0 / blk)
                kaug_ref[g, pl.ds(k0, blk), :HEAD_DIM] = kb
                kaug_ref[g, pl.ds(k0, blk), HEAD_DIM:] = hot
            return 0
        lax.fori_loop(0, nb, per_block, 0)

    own = (t0 + lax.broadcasted_iota(jnp.int32, (1, tq), 1)) // blk
    blk_id = lax.broadcasted_iota(jnp.int32, (nbp, 1), 0)
    valid = blk_id < own
    qs = []
    for g in pair:
        q = q_ref[:, g * HEAD_DIM:(g + 1) * HEAD_DIM]
        gate = lax.dot_general(kmean_ref[g], q.astype(jnp.float32), NT_DIMS, precision=HIGHEST,
                               preferred_element_type=jnp.float32)
        gate = jnp.where(valid, gate, -jnp.inf)
        rank = jnp.zeros((nbp, tq), jnp.int32)
        for m in range(nb):
            gm = gate[m:m + 1, :]
            beats = (gm > gate) | ((gm == gate) & (m < blk_id))
            rank = rank + jnp.where(beats, 1, 0)
        keep = (valid & (rank < topb)) | (blk_id == own)
        bias = jnp.where(keep, 0.0, NEG)
        bias = jnp.concatenate([bias, jnp.zeros((LANES - nbp, tq), jnp.float32)], axis=0) if nbp < LANES else bias
        qs.append(jnp.concatenate([q, bias.T.astype(jnp.bfloat16)], axis=1))

    def v_chunk(g, k0):
        return v_ref[pl.ds(k0, tq), g * HEAD_DIM:(g + 1) * HEAD_DIM]

    k0 = pl.multiple_of(i * tq, tq)
    carry = tuple(
        _flash_step(_causal_mask(_dot_nt(qs[g], kaug_ref[g, pl.ds(k0, tq), :]), tq), v_chunk(g, k0),
                    _flash_init(tq, HEAD_DIM)) for g in pair)

    def attend(c, carry):
        kk = pl.multiple_of(c * tq, tq)
        return tuple(_flash_step(_dot_nt(qs[g], kaug_ref[g, pl.ds(kk, tq), :]), v_chunk(g, kk), carry[g])
                     for g in pair)

    carry = lax.fori_loop(0, i, attend, carry)
    for g in pair:
        _, l, acc = carry[g]
        o_ref[:, g * HEAD_DIM:(g + 1) * HEAD_DIM] = (acc / l).astype(o_ref.dtype)


def moba_attention(proj, bsz, seq, col0, tq=512):
    t = proj.shape[0]
    tq = min(tq, seq)
    assert seq % tq == 0 and tq % MOBA_BLOCK == 0
    nb = seq // MOBA_BLOCK
    assert nb <= LANES
    topb = min(MOBA_TOPB_MAX, nb)
    nq = seq // tq
    nbp = -(-nb // 8) * 8
    hp = MOBA_HEADS // 2
    width = 2 * HEAD_DIM
    return pl.pallas_call(
        functools.partial(_moba_body, nb, topb),
        grid=(bsz, hp, nq),
        in_specs=[pl.BlockSpec((tq, width), lambda b, h, i: (b * nq + i, col0 * hp + h)),
                  pl.BlockSpec((seq, width), lambda b, h, i: (b, (col0 + 1) * hp + h)),
                  pl.BlockSpec((seq, width), lambda b, h, i: (b, (col0 + 2) * hp + h))],
        out_specs=pl.BlockSpec((tq, width), lambda b, h, i: (b * nq + i, h)),
        out_shape=jax.ShapeDtypeStruct((t, MOBA_HEADS * HEAD_DIM), jnp.bfloat16),
        scratch_shapes=[pltpu.VMEM((2, nbp, HEAD_DIM), jnp.float32),
                        pltpu.VMEM((2, seq, HEAD_DIM + LANES), jnp.bfloat16)],
        compiler_params=_cparams("arbitrary", "arbitrary", "arbitrary"),
        name="moba_attention",
    )(proj, proj, proj)


def _layer_norm_rows(z, g, b):
    mu = jnp.mean(z, axis=1, keepdims=True)
    dev = z - mu
    var = jnp.mean(dev * dev, axis=1, keepdims=True)
    return dev * lax.rsqrt(var + 1e-5) * g + b


def _outproj_body(a1_ref, a2_ref, w_ref, x_ref, gate_ref, lg_ref, lb_ref, o_ref, wb_ref):
    @pl.when(pl.program_id(0) == 0)
    def _():
        wb_ref[...] = w_ref[...].astype(jnp.bfloat16)

    k1 = a1_ref.shape[1]
    y = _dot(a1_ref[...], wb_ref[:k1, :]) + _dot(a2_ref[...], wb_ref[k1:, :])
    z = DEEPNORM_ALPHA * x_ref[...] + (1.0 + gate_ref[0]) * y
    o_ref[...] = _layer_norm_rows(z, lg_ref[...], lb_ref[...])


def outproj_norm(a1, c1, a2, c2, w, x2, gate, ln_g, ln_b, seq, tm=256):
    t, d = x2.shape
    kdim = w.shape[0]
    kh = kdim // 2
    spb = seq // tm
    vec = pl.BlockSpec((1, d), lambda i: (0, 0))
    return pl.pallas_call(
        _outproj_body,
        grid=(t // tm,),
        in_specs=[pl.BlockSpec((tm, kh), lambda i: (i, c1)),
                  pl.BlockSpec((tm, kh), lambda i: (i, c2)),
                  pl.BlockSpec((kdim, d), lambda i: (0, 0), pipeline_mode=pl.Buffered(1)),
                  pl.BlockSpec((tm, d), lambda i: (i, 0)),
                  pl.BlockSpec((1, 1, d), lambda i: (i // spb, 0, 0)), vec, vec],
        out_specs=pl.BlockSpec((tm, d), lambda i: (i, 0)),
        out_shape=jax.ShapeDtypeStruct((t, d), jnp.float32),
        scratch_shapes=[pltpu.VMEM((kdim, d), jnp.bfloat16)],
        compiler_params=_cparams("arbitrary"),
        name="outproj_norm",
    )(a1, a2, w, x2, gate, ln_g, ln_b)


def _rms_rows(v, g):
    return v * lax.rsqrt(jnp.mean(v * v, axis=1, keepdims=True) + 1e-6) * g


def _mla_down_body(x_ref, sc_ref, sh_ref, w_ref, qg_ref, kvg_ref, c_ref, sa_ref, sb_ref,
                   cq_ref, ckv_ref, kr_ref, wb_ref):
    @pl.when(pl.program_id(0) == 0)
    def _():
        wb_ref[...] = w_ref[...].astype(jnp.bfloat16)

    h = (x_ref[...] * (1.0 + sc_ref[0]) + sh_ref[0]).astype(jnp.bfloat16)
    acc = _dot(h, wb_ref[...])
    cq_ref[...] = _rms_rows(acc[:, :Q_LORA], qg_ref[...]).astype(jnp.bfloat16)
    ckv_ref[...] = _rms_rows(acc[:, Q_LORA:Q_LORA + KV_LORA], kvg_ref[...]).astype(jnp.bfloat16)
    lane = lax.broadcasted_iota(jnp.int32, (1, LANES), 1)
    kr = _rope_slab(acc[:, Q_LORA + KV_LORA:], c_ref[...], sa_ref[...], sb_ref[...], QK_ROPE // 2)
    kr = jnp.where(lane < QK_ROPE, kr, 0.0)
    kr_ref[0] = kr.astype(jnp.bfloat16)
    kr_ref[1] = pltpu.roll(kr, QK_ROPE, axis=1).astype(jnp.bfloat16)


def mla_down(x2, sc, sh, w_pad, q_g, kv_g, tabs, seq, tm=512):
    t, d = x2.shape
    n = w_pad.shape[1]
    spb = seq // tm
    tab_spec = pl.BlockSpec((tm, LANES), lambda i: (i % spb, 0))
    mod_spec = pl.BlockSpec((1, 1, d), lambda i: (i // spb, 0, 0))
    return pl.pallas_call(
        _mla_down_body,
        grid=(t // tm,),
        in_specs=[pl.BlockSpec((tm, d), lambda i: (i, 0)), mod_spec, mod_spec,
                  pl.BlockSpec((d, n), lambda i: (0, 0), pipeline_mode=pl.Buffered(1)),
                  pl.BlockSpec((1, Q_LORA), lambda i: (0, 0)),
                  pl.BlockSpec((1, KV_LORA), lambda i: (0, 0)),
                  tab_spec, tab_spec, tab_spec],
        out_specs=[pl.BlockSpec((tm, Q_LORA), lambda i: (i, 0)),
                   pl.BlockSpec((tm, KV_LORA), lambda i: (i, 0)),
                   pl.BlockSpec((2, tm, LANES), lambda i: (0, i, 0))],
        out_shape=[jax.ShapeDtypeStruct((t, Q_LORA), jnp.bfloat16),
                   jax.ShapeDtypeStruct((t, KV_LORA), jnp.bfloat16),
                   jax.ShapeDtypeStruct((2, t, LANES), jnp.bfloat16)],
        scratch_shapes=[pltpu.VMEM((d, n), jnp.bfloat16)],
        compiler_params=_cparams("arbitrary"),
        name="mla_down",
    )(x2, sc, sh, w_pad, q_g, kv_g, *tabs)


def _up_body(kinds, scale, a_ref, w_ref, c_ref, sa_ref, sb_ref, o_ref, wb_ref):
    j = pl.program_id(0)

    @pl.when(pl.program_id(1) == 0)
    def _():
        wb_ref[...] = w_ref[...].astype(jnp.bfloat16)

    acc = _dot(a_ref[...], wb_ref[...])
    n_slabs = acc.shape[1] // LANES

    def write(fn):
        for s in range(n_slabs):
            sl = slice(s * LANES, (s + 1) * LANES)
            o_ref[:, sl] = fn(acc[:, sl]).astype(o_ref.dtype)

    for kind in sorted(set(kinds)):
        cols = [t for t, k in enumerate(kinds) if k == kind]
        pred = functools.reduce(jnp.logical_or, [j == t for t in cols])

        @pl.when(pred)
        def _(kind=kind):
            if kind == 0:
                write(lambda a: a)
            elif kind == 4:
                write(lambda a: a * scale)
            else:
                c, sa, sb = c_ref[...], sa_ref[...], sb_ref[...]
                write(lambda a: _rope_slab(a, c, sa, sb, QK_ROPE // 2))


def up_projection(a, w, kinds, scale, tabs, seq, tm=512, tn=1024):
    t, kdim = a.shape
    n = w.shape[1]
    assert n == tn * len(kinds)
    spb = seq // tm
    tab_spec = pl.BlockSpec((tm, LANES), lambda j, i: (i % spb, 0))
    return pl.pallas_call(
        functools.partial(_up_body, tuple(kinds), scale),
        grid=(n // tn, t // tm),
        in_specs=[pl.BlockSpec((tm, kdim), lambda j, i: (i, 0)),
                  pl.BlockSpec((kdim, tn), lambda j, i: (0, j)), tab_spec, tab_spec, tab_spec],
        out_specs=pl.BlockSpec((tm, tn), lambda j, i: (i, j)),
        out_shape=jax.ShapeDtypeStruct((t, n), jnp.bfloat16),
        scratch_shapes=[pltpu.VMEM((kdim, tn), jnp.bfloat16)],
        compiler_params=_cparams("arbitrary", "arbitrary"),
        name="up_projection",
    )(a, w, *tabs)


def _causal_mask(s, tq):
    r = lax.broadcasted_iota(jnp.int32, (tq, tq), 0)
    c = lax.broadcasted_iota(jnp.int32, (tq, tq), 1)
    return jnp.where(c <= r, s, NEG)


def _mla_attn_body(qn_ref, qr_ref, kv_ref, kr_ref, o_ref, kcat_ref):
    i = pl.program_id(2)
    tq = qn_ref.shape[0]
    pair = (0, 1)
    kvw = QK_NOPE + V_DIM

    @pl.when(i == 0)
    def _():
        for g in pair:
            kcat_ref[g, :, :QK_NOPE] = kv_ref[:, g * kvw:g * kvw + QK_NOPE]
            kcat_ref[g, :, QK_NOPE:] = kr_ref[g]

    qs = [jnp.concatenate([qn_ref[:, g * QK_NOPE:(g + 1) * QK_NOPE], qr_ref[...]], axis=1) for g in pair]

    def v_chunk(g, k0):
        return kv_ref[pl.ds(k0, tq), g * kvw + QK_NOPE:(g + 1) * kvw]

    k0 = pl.multiple_of(i * tq, tq)
    carry = tuple(
        _flash_step(_causal_mask(_dot_nt(qs[g], kcat_ref[g, pl.ds(k0, tq), :]), tq), v_chunk(g, k0),
                    _flash_init(tq, V_DIM)) for g in pair)

    def attend(c, carry):
        kk = pl.multiple_of(c * tq, tq)
        return tuple(_flash_step(_dot_nt(qs[g], kcat_ref[g, pl.ds(kk, tq), :]), v_chunk(g, kk), carry[g])
                     for g in pair)

    carry = lax.fori_loop(0, i, attend, carry)
    for g in pair:
        _, l, acc = carry[g]
        o_ref[:, g * V_DIM:(g + 1) * V_DIM] = (acc / l).astype(o_ref.dtype)


def mla_attention(qarr, kv, kr2, bsz, seq, tq=512):
    t = qarr.shape[0]
    tq = min(tq, seq)
    nq = seq // tq
    hp = MLA_HEADS // 2
    return pl.pallas_call(
        _mla_attn_body,
        grid=(bsz, hp, nq),
        in_specs=[pl.BlockSpec((tq, 2 * QK_NOPE), lambda b, h, i: (b * nq + i, h)),
                  pl.BlockSpec((tq, LANES), lambda b, h, i: (b * nq + i, MLA_HEADS + h)),
                  pl.BlockSpec((seq, 2 * (QK_NOPE + V_DIM)), lambda b, h, i: (b, h)),
                  pl.BlockSpec((2, seq, LANES), lambda b, h, i: (0, b, 0))],
        out_specs=pl.BlockSpec((tq, 2 * V_DIM), lambda b, h, i: (b * nq + i, h)),
        out_shape=jax.ShapeDtypeStruct((t, MLA_HEADS * V_DIM), jnp.bfloat16),
        scratch_shapes=[pltpu.VMEM((2, seq, QK_NOPE + LANES), jnp.bfloat16)],
        compiler_params=_cparams("arbitrary", "arbitrary", "arbitrary"),
        name="mla_attention",
    )(qarr, qarr, kv, kr2)


def _pack_halves(v):
    half = v.shape[1] // 2
    bits = pltpu.bitcast(v.astype(jnp.bfloat16).astype(jnp.float32), jnp.int32)
    return lax.shift_right_logical(bits[:, :half], 16) | bits[:, half:]


def _unpack_halves(p):
    lo = pltpu.bitcast(lax.shift_left(p, 16), jnp.float32)
    hi = pltpu.bitcast(p & -65536, jnp.float32)
    return lo, hi


def _router_body(x_ref, sc_ref, sh_ref, wr_ref, br_ref, hp_ref, idx_ref, gate_ref, rank_ref, cnt_ref, run_ref):
    @pl.when(pl.program_id(0) == 0)
    def _():
        run_ref[...] = jnp.zeros_like(run_ref)

    h = x_ref[...] * (1.0 + sc_ref[0]) + sh_ref[0]
    tm = h.shape[0]
    hp_ref[...] = _pack_halves(h)
    logits = jnp.dot(h, wr_ref[...], precision=HIGHEST, preferred_element_type=jnp.float32) + br_ref[...]
    lane = lax.broadcasted_iota(jnp.int32, (1, LANES), 1).astype(jnp.float32)
    vals, idxs, hots = [], [], []
    rest = logits
    for _ in range(TOP_K):
        m = jnp.max(rest, axis=1, keepdims=True)
        idx = jnp.min(jnp.where(rest == m, lane, float(LANES)), axis=1, keepdims=True)
        hot = lane == idx
        vals.append(m)
        idxs.append(idx)
        hots.append(hot)
        rest = jnp.where(hot, -jnp.inf, rest)
    exps = [jnp.exp(v - vals[0]) for v in vals]
    den = functools.reduce(jnp.add, exps)
    hot_all = functools.reduce(jnp.add, [jnp.where(hot, 1.0, 0.0) for hot in hots])
    r = lax.broadcasted_iota(jnp.int32, (tm, tm), 0)
    c = lax.broadcasted_iota(jnp.int32, (tm, tm), 1)
    below = jnp.where(c < r, 1.0, 0.0).astype(jnp.bfloat16)
    before = _dot(below, hot_all.astype(jnp.bfloat16)) + run_ref[0:1, :]
    idx_out = jnp.zeros((tm, LANES), jnp.float32)
    gate_out = jnp.zeros((tm, LANES), jnp.float32)
    rank_out = jnp.zeros((tm, LANES), jnp.float32)
    for k in range(TOP_K):
        slot = lane == float(k)
        rank_k = jnp.sum(jnp.where(hots[k], before, 0.0), axis=1, keepdims=True)
        idx_out = jnp.where(slot, idxs[k], idx_out)
        gate_out = jnp.where(slot, exps[k] / den, gate_out)
        rank_out = jnp.where(slot, rank_k, rank_out)
    idx_ref[...] = idx_out.astype(jnp.int32)
    gate_ref[...] = gate_out
    rank_ref[...] = rank_out.astype(jnp.int32)
    run_ref[...] = run_ref[...] + jnp.sum(hot_all, axis=0, keepdims=True)
    cnt_ref[...] = run_ref[...].astype(jnp.int32)


def moe_router(x2, sc, sh, w_r, b_r, seq, tm=512):
    t, d = x2.shape
    e = w_r.shape[1]
    spb = seq // tm
    wr = jnp.pad(w_r, ((0, 0), (0, LANES - e)))
    br = jnp.pad(b_r, (0, LANES - e), constant_values=NEG)[None]
    mod_spec = pl.BlockSpec((1, 1, d), lambda i: (i // spb, 0, 0))
    row = pl.BlockSpec((tm, LANES), lambda i: (i, 0))
    return pl.pallas_call(
        _router_body,
        grid=(t // tm,),
        in_specs=[pl.BlockSpec((tm, d), lambda i: (i, 0)), mod_spec, mod_spec,
                  pl.BlockSpec((d, LANES), lambda i: (0, 0)),
                  pl.BlockSpec((1, LANES), lambda i: (0, 0))],
        out_specs=[pl.BlockSpec((tm, d // 2), lambda i: (i, 0)), row, row, row,
                   pl.BlockSpec((8, LANES), lambda i: (0, 0))],
        out_shape=[jax.ShapeDtypeStruct((t, d // 2), jnp.int32),
                   jax.ShapeDtypeStruct((t, LANES), jnp.int32),
                   jax.ShapeDtypeStruct((t, LANES), jnp.float32),
                   jax.ShapeDtypeStruct((t, LANES), jnp.int32),
                   jax.ShapeDtypeStruct((8, LANES), jnp.int32)],
        scratch_shapes=[pltpu.VMEM((8, LANES), jnp.float32)],
        compiler_params=_cparams("arbitrary"),
        name="moe_router",
    )(x2, sc, sh, wr, br)


def _row_gather_start(src_hbm, dst, sem, rows, row_of):
    def body(r, _):
        pltpu.make_async_copy(src_hbm.at[pl.ds(row_of(r), 1), :], dst.at[pl.ds(r, 1), :], sem).start()
        return 0
    lax.fori_loop(0, rows, body, 0)


def _row_gather_wait(src_hbm, dst, sem, rows):
    pltpu.make_async_copy(src_hbm.at[pl.ds(0, rows), :], dst, sem).wait()


def _experts_body(te_ref, src_ref, nu_ref, h_hbm, wgu_ref, bgu_ref, wdn_ref, bdn_ref, o_ref,
                  xbuf, sem, wgu_b, wdn_x, wdn_b):
    j = pl.program_id(0)
    nused = nu_ref[0]
    tmg = xbuf.shape[1]
    f = wdn_ref.shape[1]

    @pl.when(j == 0)
    def _():
        wdn_x[...] = jnp.zeros_like(wdn_x)

    def issue(tile, slot):
        _row_gather_start(h_hbm, xbuf.at[slot], sem.at[slot], tmg, lambda r: src_ref[tile * tmg + r])

    @pl.when(j == 0)
    def _():
        issue(0, 0)

    @pl.when(j + 1 < nused)
    def _():
        issue(j + 1, (j + 1) % 2)

    @pl.when(j < nused)
    def _():
        slot = j % 2
        _row_gather_wait(h_hbm, xbuf.at[slot], sem.at[slot], tmg)

        @pl.when((j == 0) | (te_ref[j] != te_ref[jnp.maximum(j - 1, 0)]))
        def _():
            wgu_b[...] = wgu_ref[0].astype(jnp.bfloat16)
            for s in range(wdn_x.shape[0]):
                cols = slice(s * LANES, (s + 1) * LANES)
                wdn_x[s, pl.ds(0, f, stride=2), :] = wdn_ref[0, :, cols]
                wdn_b[:, cols] = wdn_x[s].astype(jnp.bfloat16)

        lo, hi = _unpack_halves(xbuf[slot])
        half = lo.shape[1]
        gu = (_dot(lo.astype(jnp.bfloat16), wgu_b[:half, :])
              + _dot(hi.astype(jnp.bfloat16), wgu_b[half:, :]) + bgu_ref[0])
        acts = []
        for s in range(gu.shape[1] // LANES):
            g = gu[:, s * LANES:(s + 1) * LANES]
            glu = jnp.minimum(g, SWIGLU_LIMIT)
            lin = jnp.clip(g, -SWIGLU_LIMIT, SWIGLU_LIMIT) + 1.0
            act = glu * jax.nn.sigmoid(SWIGLU_ALPHA * glu) * pltpu.roll(lin, LANES - 1, axis=1)
            acts.append(act.astype(jnp.bfloat16))
        y = _dot(jnp.concatenate(acts, axis=1), wdn_b[...]) + bdn_ref[0]
        o_ref[...] = _pack_halves(y)

    @pl.when(j >= nused)
    def _():
        o_ref[...] = jnp.zeros_like(o_ref)


def moe_experts(hp, tile_expert, row_src, nused, wgu, bgu, wdn, bdn, tmg):
    t, half = hp.shape
    d = 2 * half
    nt = tile_expert.shape[0]
    e, _, f2 = wgu.shape
    grid_spec = pltpu.PrefetchScalarGridSpec(
        num_scalar_prefetch=3,
        grid=(nt,),
        in_specs=[pl.BlockSpec(memory_space=pl.ANY),
                  pl.BlockSpec((1, d, f2), lambda j, te, src, nu: (te[j], 0, 0)),
                  pl.BlockSpec((1, 1, f2), lambda j, te, src, nu: (te[j], 0, 0)),
                  pl.BlockSpec((1, f2 // 2, d), lambda j, te, src, nu: (te[j], 0, 0)),
                  pl.BlockSpec((1, 1, d), lambda j, te, src, nu: (te[j], 0, 0))],
        out_specs=pl.BlockSpec((tmg, half), lambda j, te, src, nu: (j, 0)),
        scratch_shapes=[pltpu.VMEM((2, tmg, half), jnp.int32),
                        pltpu.SemaphoreType.DMA((2,)),
                        pltpu.VMEM((d, f2), jnp.bfloat16),
                        pltpu.VMEM((d // LANES, f2, LANES), jnp.float32),
                        pltpu.VMEM((f2, d), jnp.bfloat16)])
    return pl.pallas_call(
        _experts_body,
        grid_spec=grid_spec,
        out_shape=jax.ShapeDtypeStruct((nt * tmg, half), jnp.int32),
        compiler_params=_cparams("arbitrary"),
        name="moe_experts",
    )(tile_expert, row_src, nused, hp, wgu, bgu, wdn, bdn)


def _combine_body(dest_ref, og_hbm, gate_ref, x_ref, g2_ref, lg_ref, lb_ref, o_ref, buf, sem):
    i = pl.program_id(0)
    tmc = x_ref.shape[0]

    def issue(tile, slot):
        for k in range(TOP_K):
            _row_gather_start(og_hbm, buf.at[slot, k], sem.at[slot], tmc,
                              lambda r, k=k: dest_ref[(tile * tmc + r) * TOP_K + k])

    @pl.when(i == 0)
    def _():
        issue(0, 0)

    @pl.when(i + 1 < pl.num_programs(0))
    def _():
        issue(i + 1, (i + 1) % 2)

    slot = i % 2
    for k in range(TOP_K):
        _row_gather_wait(og_hbm, buf.at[slot, k], sem.at[slot], tmc)
    gate = gate_ref[...]
    y_lo = y_hi = None
    for k in range(TOP_K):
        lo, hi = _unpack_halves(buf[slot, k])
        gk = gate[:, k:k + 1]
        y_lo = gk * lo if y_lo is None else y_lo + gk * lo
        y_hi = gk * hi if y_hi is None else y_hi + gk * hi
    y = jnp.concatenate([y_lo, y_hi], axis=1)
    z = DEEPNORM_ALPHA * x_ref[...] + (1.0 + g2_ref[0]) * y
    o_ref[...] = _layer_norm_rows(z, lg_ref[...], lb_ref[...])


def moe_combine_norm(og, dest_flat, gates, x2, g2, ln_g, ln_b, seq, tmc=256):
    t, d = x2.shape
    spb = seq // tmc
    vec = pl.BlockSpec((1, d), lambda i, dst: (0, 0))
    grid_spec = pltpu.PrefetchScalarGridSpec(
        num_scalar_prefetch=1,
        grid=(t // tmc,),
        in_specs=[pl.BlockSpec(memory_space=pl.ANY),
                  pl.BlockSpec((tmc, LANES), lambda i, dst: (i, 0)),
                  pl.BlockSpec((tmc, d), lambda i, dst: (i, 0)),
                  pl.BlockSpec((1, 1, d), lambda i, dst: (i // spb, 0, 0)), vec, vec],
        out_specs=pl.BlockSpec((tmc, d), lambda i, dst: (i, 0)),
        scratch_shapes=[pltpu.VMEM((2, TOP_K, tmc, d // 2), jnp.int32),
                        pltpu.SemaphoreType.DMA((2,))])
    return pl.pallas_call(
        _combine_body,
        grid_spec=grid_spec,
        out_shape=jax.ShapeDtypeStruct((t, d), jnp.float32),
        compiler_params=_cparams("arbitrary"),
        name="moe_combine_norm",
    )(dest_flat, og, gates, x2, g2, ln_g, ln_b)


def moe_layer(x2, sc, sh, g2, ln_g, ln_b, w_r, b_r, w_gu, b_gu, w_dn, b_dn, seq, tmg=256):
    t, d = x2.shape
    e = w_r.shape[1]
    f = w_dn.shape[1]
    hp, idx, gates, rank, cnt = moe_router(x2, sc, sh, w_r, b_r, seq)
    idx, rank, counts = idx[:, :TOP_K], rank[:, :TOP_K], cnt[0, :e]
    tiles = (counts + tmg - 1) // tmg
    cum = jnp.cumsum(tiles)
    dest = ((cum - tiles)[idx] * tmg + rank).reshape(-1)
    nt = (t * TOP_K) // tmg + e
    tile_ids = jnp.arange(nt, dtype=jnp.int32)
    tile_expert = jnp.minimum(jnp.sum((cum[None, :] <= tile_ids[:, None]).astype(jnp.int32), axis=1), e - 1)
    row_src = jnp.zeros((nt * tmg,), jnp.int32).at[dest].set(jnp.repeat(jnp.arange(t, dtype=jnp.int32), TOP_K))
    og = moe_experts(hp, tile_expert.astype(jnp.int32), row_src, cum[-1:].astype(jnp.int32),
                     w_gu, b_gu[:, None, :], w_dn, b_dn[:, None, :], tmg)
    return moe_combine_norm(og, dest.astype(jnp.int32), gates, x2, g2, ln_g, ln_b, seq)


def mla_mixer_core(x2, sc, sh, w_down, q_norm, kv_norm, w_q_up, w_kv_up, bsz, seq):
    d = x2.shape[1]
    pad = (-w_down.shape[1]) % LANES
    w_pad = jnp.pad(w_down, ((0, 0), (0, pad)))
    k_tabs = rope_tables(seq, QK_ROPE, 1.0)
    scale = (QK_NOPE + QK_ROPE) ** -0.5
    q_tabs = rope_tables(seq, QK_ROPE, scale)
    cq, ckv, kr2 = mla_down(x2, sc, sh, w_pad, q_norm[None], kv_norm[None], k_tabs, seq)
    wq = w_q_up.reshape(Q_LORA, MLA_HEADS, QK_NOPE + QK_ROPE)
    wq = jnp.concatenate([wq[:, :, :QK_NOPE].reshape(Q_LORA, -1), wq[:, :, QK_NOPE:].reshape(Q_LORA, -1)], axis=1)
    qarr = up_projection(cq, wq, (4, 4, 3), scale, q_tabs, seq)
    kv = up_projection(ckv, w_kv_up, (0, 0, 0, 0), scale, q_tabs, seq)
    return mla_attention(qarr, kv, kr2, bsz, seq)


def even_mixer_core(x2, sc, sh, w_in, idx_g, idx_b, tabs, ki_tabs, bsz, seq):
    a_cols = 3 * DSA_HEADS * HEAD_DIM + IDX_HEADS * IDX_DIM
    i_cols = IDX_DIM + IDX_HEADS
    w_main = jnp.concatenate([w_in[:, :a_cols], w_in[:, a_cols + i_cols:]], axis=1)
    w_idx = jnp.pad(w_in[:, a_cols:a_cols + i_cols], ((0, 0), (0, LANES - i_cols)))
    proj = modulated_projection(x2, sc, sh, w_main, EVEN_KINDS, tabs, seq)
    klo, khi, w16 = idx_projection(x2, sc, sh, w_idx, jnp.pad(idx_g, (0, LANES - IDX_DIM))[None],
                                   jnp.pad(idx_b, (0, LANES - IDX_DIM))[None], ki_tabs, seq)
    out_a = dsa_attention(proj, klo, khi, w16, bsz, seq)
    out_b = moba_attention(proj, bsz, seq, 4)
    return out_a, out_b


@jax.jit
def _forward(x, c, w_mod, b_mod, ln1_g, ln1_b, ln2_g, ln2_b, even_w_in, even_w_out, idx_ln_g, idx_ln_b,
             mla_w_down, mla_q_norm, mla_kv_norm, mla_w_q_up, mla_w_kv_up, mla_w_out, router_w, router_b,
             exp_w_gu, exp_b_gu, exp_w_down, exp_b_down):
    bsz, seq, d = x.shape
    depth = w_mod.shape[0]
    mods = mod_vectors(c, w_mod, b_mod).reshape(depth, bsz, N_MOD, 1, d)
    tabs = even_tables(seq)
    ki_tabs = rope_tables(seq, IDX_DIM, 1.0)
    x2 = x.reshape(bsz * seq, d)
    for l in range(depth):
        sh1, sc1, g1, sh2, sc2, g2 = [mods[l, :, m] for m in range(N_MOD)]
        j = l // 2
        if l % 2 == 0:
            out_a, out_b = even_mixer_core(x2, sc1, sh1, even_w_in[j], idx_ln_g[j], idx_ln_b[j],
                                           tabs, ki_tabs, bsz, seq)
            x2 = outproj_norm(out_a, 0, out_b, 0, even_w_out[j], x2, g1, ln1_g[l][None], ln1_b[l][None], seq)
        else:
            attn = mla_mixer_core(x2, sc1, sh1, mla_w_down[j], mla_q_norm[j], mla_kv_norm[j],
                                  mla_w_q_up[j], mla_w_kv_up[j], bsz, seq)
            x2 = outproj_norm(attn, 0, attn, 1, mla_w_out[j], x2, g1, ln1_g[l][None], ln1_b[l][None], seq)
        x2 = moe_layer(x2, sc2, sh2, g2, ln2_g[l][None], ln2_b[l][None], router_w[l], router_b[l],
                       exp_w_gu[l], exp_b_gu[l], exp_w_down[l], exp_b_down[l], seq)
    return x2.reshape(bsz, seq, d)


def kernel(x, c, w_mod, b_mod, ln1_g, ln1_b, ln2_g, ln2_b, even_w_in, even_w_out, idx_ln_g, idx_ln_b,
           mla_w_down, mla_q_norm, mla_kv_norm, mla_w_q_up, mla_w_kv_up, mla_w_out, router_w, router_b,
           exp_w_gu, exp_b_gu, exp_w_down, exp_b_down):
    return _forward(x, c, w_mod, b_mod, ln1_g, ln1_b, ln2_g, ln2_b, even_w_in, even_w_out, idx_ln_g,
                    idx_ln_b, mla_w_down, mla_q_norm, mla_kv_norm, mla_w_q_up, mla_w_kv_up, mla_w_out,
                    router_w, router_b, exp_w_gu, exp_b_gu, exp_w_down, exp_b_down)
```
